```python
import math
import jax, jax.numpy as jnp
from jax import lax
import numpy as np

D_MODEL = 2048
BATCH = 4
SEQ = 2048
DEPTH = 2
DEC_BATCH = 128
DEC_SEQ = 1
PAST_LEN = 16384
PAGE_SIZE = 128

D_FF = 5632
EPS = 1e-6
N_MIXERS = 2
MLA_HEADS = 16
Q_LORA = 512
KV_LORA = 512
QK_NOPE = 128
QK_ROPE = 64
V_HEAD = 128
MLA_IN = Q_LORA + KV_LORA + QK_ROPE
MLA_SCALE = (QK_NOPE + QK_ROPE) ** -0.5
ROPE_THETA = 10000.0
DIFF_HEADS = 16
DIFF_KV_HEADS = 2
DIFF_GROUP = DIFF_HEADS // DIFF_KV_HEADS
DIFF_HEAD = 64
DIFF_VHEAD = 2 * DIFF_HEAD
DIFF_Q = DIFF_HEADS * 2 * DIFF_HEAD
DIFF_K = DIFF_KV_HEADS * 2 * DIFF_HEAD
DIFF_V = DIFF_KV_HEADS * DIFF_VHEAD
DIFF_SCALE = DIFF_HEAD ** -0.5
NUM_BUCKETS = 32
MAX_DISTANCE = 128
Q_BLOCK = 128
PAGES_PER_STEP_MAX = 8
NEG_INF = -1e30
N_MLA_LAYERS = (DEPTH + 1) // 2
N_DIFF_LAYERS = DEPTH // 2

kernel_name = 'hybrid_mla_diffattn_macaron_step'


def rmsnorm(x, g):
    xf = x.astype(jnp.float32)
    y = xf * lax.rsqrt(jnp.mean(xf * xf, axis=-1, keepdims=True) + EPS)
    return y.astype(x.dtype) * g


def swiglu(h, w_up, w_down):
    gate, up = jnp.split(h @ w_up, 2, axis=-1)
    return (jax.nn.silu(gate) * up) @ w_down


def rope_cos_sin(pos):
    inv = ROPE_THETA ** (-jnp.arange(0, QK_ROPE, 2, dtype=jnp.float32) / QK_ROPE)
    ang = pos.astype(jnp.float32)[:, None] * inv[None, :]
    return jnp.cos(ang), jnp.sin(ang)


def apply_rope(x, cos, sin):
    xf = x.astype(jnp.float32)
    x1, x2 = xf[..., :QK_ROPE // 2], xf[..., QK_ROPE // 2:]
    return jnp.concatenate([x1 * cos - x2 * sin, x2 * cos + x1 * sin], axis=-1).astype(x.dtype)


def t5_bias(qpos, kpos, rel_bias):
    n = jnp.maximum(qpos[:, None] - kpos[None, :], 0)
    max_exact = NUM_BUCKETS // 2
    nf = jnp.maximum(n, 1).astype(jnp.float32)
    large = max_exact + (jnp.log(nf / max_exact) / math.log(MAX_DISTANCE / max_exact)
                         * (NUM_BUCKETS - max_exact)).astype(jnp.int32)
    bucket = jnp.where(n < max_exact, n, jnp.minimum(large, NUM_BUCKETS - 1))
    return jnp.transpose(rel_bias[bucket], (2, 0, 1)).astype(jnp.float32)


def online_update(state, s, values_of_p):
    m, l, acc = state
    m_new = jnp.maximum(m, jnp.max(s, axis=-1))
    corr = jnp.exp(m - m_new)
    p = jnp.exp(s - m_new[..., None])
    return (m_new, l * corr + jnp.sum(p, axis=-1), acc * corr[..., None] + values_of_p(p))


def page_blocks(page_table):
    n_pages = PAST_LEN // PAGE_SIZE
    ppb = math.gcd(n_pages, PAGES_PER_STEP_MAX)
    nb = n_pages // ppb
    blocks = page_table.T.reshape(nb, ppb, page_table.shape[0])
    starts = jnp.arange(nb, dtype=jnp.int32) * (ppb * PAGE_SIZE)
    return blocks, starts, ppb * PAGE_SIZE


def mla_project(h, pos, w_in, q_norm, kv_norm, w_q_up):
    b, t, _ = h.shape
    proj = h @ w_in
    q_c = rmsnorm(proj[..., :Q_LORA], q_norm)
    ckv = rmsnorm(proj[..., Q_LORA:Q_LORA + KV_LORA], kv_norm)
    kr = proj[..., Q_LORA + KV_LORA:]
    q = (q_c @ w_q_up).reshape(b, t, MLA_HEADS, QK_NOPE + QK_ROPE)
    cos, sin = rope_cos_sin(pos)
    q_rope = apply_rope(q[..., QK_NOPE:], cos[:, None, :], sin[:, None, :])
    kr = apply_rope(kr, cos, sin)
    return q[..., :QK_NOPE], q_rope, ckv, kr


def mla_split_up(w_kv_up):
    w = w_kv_up.reshape(KV_LORA, MLA_HEADS, QK_NOPE + V_HEAD)
    return w[..., :QK_NOPE], w[..., QK_NOPE:]


def mla_scores(ql, qr, ck, kr):
    s = (jnp.einsum('bqhc,bkc->bhqk', ql, ck, preferred_element_type=jnp.float32)
         + jnp.einsum('bqhr,bkr->bhqk', qr, kr, preferred_element_type=jnp.float32))
    return s * MLA_SCALE


def mla_values(p, ck):
    return jnp.einsum('bhqk,bkc->bhqc', p.astype(ck.dtype), ck, preferred_element_type=jnp.float32)


def mla_finish(o_lat, w_uv, w_out, dtype):
    b, _, t, _ = o_lat.shape
    o = jnp.einsum('bhtc,chv->bthv', o_lat.astype(dtype), w_uv)
    return o.reshape(b, t, MLA_HEADS * V_HEAD) @ w_out


def mla_prompt(h, w_in, q_norm, kv_norm, w_q_up, w_kv_up, w_out):
    b, s, _ = h.shape
    pos = jnp.arange(s, dtype=jnp.int32)
    q_nope, q_rope, ckv, kr = mla_project(h, pos, w_in, q_norm, kv_norm, w_q_up)
    w_uk, w_uv = mla_split_up(w_kv_up)
    q_lat = jnp.einsum('bshn,chn->bshc', q_nope, w_uk)
    nqb = s // Q_BLOCK
    qlb = q_lat.reshape(b, nqb, Q_BLOCK, MLA_HEADS, KV_LORA).swapaxes(0, 1)
    qrb = q_rope.reshape(b, nqb, Q_BLOCK, MLA_HEADS, QK_ROPE).swapaxes(0, 1)
    starts = jnp.arange(nqb, dtype=jnp.int32) * Q_BLOCK

    def block(args):
        ql, qr, i0 = args
        sc = mla_scores(ql, qr, ckv, kr)
        qpos = i0 + jnp.arange(Q_BLOCK, dtype=jnp.int32)
        sc = jnp.where(pos[None, :] <= qpos[:, None], sc, NEG_INF)
        return mla_values(jax.nn.softmax(sc, axis=-1), ckv)

    o = lax.map(block, (qlb, qrb, starts))
    o = o.transpose(1, 2, 0, 3, 4).reshape(b, MLA_HEADS, s, KV_LORA)
    return mla_finish(o, w_uv, w_out, h.dtype), ckv, kr


def mla_sample(h, cache_ckv, cache_kr, j, page_table, w_in, q_norm, kv_norm, w_q_up, w_kv_up, w_out):
    b, t, _ = h.shape
    qpos = PAST_LEN + jnp.arange(t, dtype=jnp.int32)
    q_nope, q_rope, ckv_new, kr_new = mla_project(h, qpos, w_in, q_norm, kv_norm, w_q_up)
    w_uk, w_uv = mla_split_up(w_kv_up)
    q_lat = jnp.einsum('bshn,chn->bshc', q_nope, w_uk)
    blocks, starts, kb = page_blocks(page_table)
    init = (jnp.full((b, MLA_HEADS, t), NEG_INF, jnp.float32),
            jnp.zeros((b, MLA_HEADS, t), jnp.float32),
            jnp.zeros((b, MLA_HEADS, t, KV_LORA), jnp.float32))

    def step(state, pages):
        idx = pages.T
        ck = cache_ckv[j, idx].reshape(b, kb, KV_LORA)
        kr = cache_kr[j, idx].reshape(b, kb, QK_ROPE)
        sc = mla_scores(q_lat, q_rope, ck, kr)
        return online_update(state, sc, lambda p: mla_values(p, ck)), None

    state, _ = lax.scan(step, init, blocks)
    sc = mla_scores(q_lat, q_rope, ckv_new, kr_new)
    sc = jnp.where(qpos[None, :] <= qpos[:, None], sc, NEG_INF)
    _, l, acc = online_update(state, sc, lambda p: mla_values(p, ckv_new))
    return mla_finish(acc / l[..., None], w_uv, w_out, h.dtype), ckv_new, kr_new


def diff_project(h, w_in):
    b, t, _ = h.shape
    proj = h @ w_in
    q = proj[..., :DIFF_Q].reshape(b, t, DIFF_KV_HEADS, DIFF_GROUP, 2, DIFF_HEAD)
    k = proj[..., DIFF_Q:DIFF_Q + DIFF_K].reshape(b, t, DIFF_KV_HEADS, 2, DIFF_HEAD)
    v = proj[..., DIFF_Q + DIFF_K:].reshape(b, t, DIFF_KV_HEADS, DIFF_VHEAD)
    return q, k, v


def diff_scores(q, k, bias):
    b, tq = q.shape[0], q.shape[1]
    tk = k.shape[1]
    s = jnp.einsum('bqhgmd,bkhmd->bmhgqk', q, k, preferred_element_type=jnp.float32)
    return s.reshape(b, 2, DIFF_HEADS, tq, tk) * DIFF_SCALE + bias[None, None]


def diff_values(p, v):
    b, _, _, tq, tk = p.shape
    pg = p.reshape(b, 2, DIFF_KV_HEADS, DIFF_GROUP, tq, tk).astype(v.dtype)
    o = jnp.einsum('bmhgqk,bkhe->bmhgqe', pg, v, preferred_element_type=jnp.float32)
    return o.reshape(b, 2, DIFF_HEADS, tq, DIFF_VHEAD)


def diff_finish(o, layer_idx, lam, subln, w_out, dtype):
    lam_init = 0.8 - 0.6 * math.exp(-0.3 * layer_idx)
    lf = lam.astype(jnp.float32)
    lam_full = jnp.exp(jnp.sum(lf[0] * lf[1])) - jnp.exp(jnp.sum(lf[2] * lf[3])) + lam_init
    a = (o[:, 0] - lam_full * o[:, 1]).transpose(0, 2, 1, 3)
    a = rmsnorm(a, subln) * (1.0 - lam_init)
    b, t = a.shape[0], a.shape[1]
    return a.astype(dtype).reshape(b, t, DIFF_HEADS * DIFF_VHEAD) @ w_out


def diff_prompt(h, layer_idx, w_in, lam, subln, w_out, rel_bias):
    b, s, _ = h.shape
    pos = jnp.arange(s, dtype=jnp.int32)
    q, k, v = diff_project(h, w_in)
    nqb = s // Q_BLOCK
    qb = q.reshape(b, nqb, Q_BLOCK, DIFF_KV_HEADS, DIFF_GROUP, 2, DIFF_HEAD).swapaxes(0, 1)
    starts = jnp.arange(nqb, dtype=jnp.int32) * Q_BLOCK

    def block(args):
        qq, i0 = args
        qpos = i0 + jnp.arange(Q_BLOCK, dtype=jnp.int32)
        sc = diff_scores(qq, k, t5_bias(qpos, pos, rel_bias))
        sc = jnp.where(pos[None, :] <= qpos[:, None], sc, NEG_INF)
        return diff_values(jax.nn.softmax(sc, axis=-1), v)

    o = lax.map(block, (qb, starts))
    o = o.transpose(1, 2, 3, 0, 4, 5).reshape(b, 2, DIFF_HEADS, s, DIFF_VHEAD)
    return diff_finish(o, layer_idx, lam, subln, w_out, h.dtype), k, v


def diff_sample(h, layer_idx, cache_k, cache_v, j, page_table, w_in, lam, subln, w_out, rel_bias):
    b, t, _ = h.shape
    qpos = PAST_LEN + jnp.arange(t, dtype=jnp.int32)
    q, k_new, v_new = diff_project(h, w_in)
    blocks, starts, kb = page_blocks(page_table)
    init = (jnp.full((b, 2, DIFF_HEADS, t), NEG_INF, jnp.float32),
            jnp.zeros((b, 2, DIFF_HEADS, t), jnp.float32),
            jnp.zeros((b, 2, DIFF_HEADS, t, DIFF_VHEAD), jnp.float32))

    def step(state, xs):
        pages, k0 = xs
        idx = pages.T
        kc = cache_k[j, idx].reshape(b, kb, DIFF_KV_HEADS, 2, DIFF_HEAD)
        vc = cache_v[j, idx].reshape(b, kb, DIFF_KV_HEADS, DIFF_VHEAD)
        kpos = k0 + jnp.arange(kb, dtype=jnp.int32)
        sc = diff_scores(q, kc, t5_bias(qpos, kpos, rel_bias))
        return online_update(state, sc, lambda p: diff_values(p, vc)), None

    state, _ = lax.scan(step, init, (blocks, starts))
    sc = diff_scores(q, k_new, t5_bias(qpos, qpos, rel_bias))
    sc = jnp.where(qpos[None, :] <= qpos[:, None], sc, NEG_INF)
    _, l, acc = online_update(state, sc, lambda p: diff_values(p, v_new))
    return diff_finish(acc / l[..., None], layer_idx, lam, subln, w_out, h.dtype), k_new, v_new


def setup_inputs(seed: int = 0) -> dict:
    key = jax.random.key(seed)
    ks = iter(jax.random.split(key, 32))

    def nrm(shape, scale):
        return jax.random.normal(next(ks), shape, jnp.float32) * scale

    def gain(shape):
        return 1.0 + nrm(shape, 0.02)

    n_pages = PAST_LEN // PAGE_SIZE
    n_used = DEC_BATCH * n_pages
    n_phys = n_used + max(1, n_used // 4)
    page_table = jax.random.permutation(next(ks), n_phys)[:n_used].reshape(DEC_BATCH, n_pages).astype(jnp.int32)
    return {
        'x_prompt': nrm((BATCH, SEQ, D_MODEL), 1.0),
        'x_sample': nrm((DEC_BATCH, DEC_SEQ, D_MODEL), 1.0),
        'cache_mla_ckv': nrm((N_MLA_LAYERS, n_phys, PAGE_SIZE, KV_LORA), 1.0),
        'cache_mla_krope': nrm((N_MLA_LAYERS, n_phys, PAGE_SIZE, QK_ROPE), 1.0),
        'cache_diff_k': nrm((N_DIFF_LAYERS, n_phys, PAGE_SIZE, DIFF_KV_HEADS, 2, DIFF_HEAD), 1.0),
        'cache_diff_v': nrm((N_DIFF_LAYERS, n_phys, PAGE_SIZE, DIFF_KV_HEADS, DIFF_VHEAD), 1.0),
        'page_table': page_table,
        'ffn_norm': gain((DEPTH, 2, D_MODEL)),
        'w_ffn_up': nrm((DEPTH, 2, D_MODEL, 2 * D_FF), D_MODEL ** -0.5),
        'w_ffn_down': nrm((DEPTH, 2, D_FF, D_MODEL), D_FF ** -0.5),
        'mix_norm': gain((DEPTH, D_MODEL)),
        'mla_w_in': nrm((N_MLA_LAYERS, D_MODEL, MLA_IN), D_MODEL ** -0.5),
        'mla_q_norm': gain((N_MLA_LAYERS, Q_LORA)),
        'mla_kv_norm': gain((N_MLA_LAYERS, KV_LORA)),
        'mla_w_q_up': nrm((N_MLA_LAYERS, Q_LORA, MLA_HEADS * (QK_NOPE + QK_ROPE)), Q_LORA ** -0.5),
        'mla_w_kv_up': nrm((N_MLA_LAYERS, KV_LORA, MLA_HEADS * (QK_NOPE + V_HEAD)), KV_LORA ** -0.5),
        'mla_w_out': nrm((N_MLA_LAYERS, MLA_HEADS * V_HEAD, D_MODEL), (MLA_HEADS * V_HEAD) ** -0.5),
        'diff_w_in': nrm((N_DIFF_LAYERS, D_MODEL, DIFF_Q + DIFF_K + DIFF_V), D_MODEL ** -0.5),
        'diff_lambda': nrm((N_DIFF_LAYERS, 4, DIFF_HEAD), 0.1),
        'diff_subln': gain((N_DIFF_LAYERS, DIFF_VHEAD)),
        'diff_w_out': nrm((N_DIFF_LAYERS, DIFF_HEADS * DIFF_VHEAD, D_MODEL), (DIFF_HEADS * DIFF_VHEAD) ** -0.5),
        'rel_bias': nrm((NUM_BUCKETS, DIFF_HEADS), 0.5),
        'final_norm': gain((D_MODEL,)),
    }


def reference(x_prompt, x_sample, cache_mla_ckv, cache_mla_krope, cache_diff_k, cache_diff_v, page_table,
              ffn_norm, w_ffn_up, w_ffn_down, mix_norm,
              mla_w_in, mla_q_norm, mla_kv_norm, mla_w_q_up, mla_w_kv_up, mla_w_out,
              diff_w_in, diff_lambda, diff_subln, diff_w_out, rel_bias, final_norm):
    xp, xs = x_prompt, x_sample
    p_ckv, p_kr, p_k, p_v = [], [], [], []
    s_ckv, s_kr, s_k, s_v = [], [], [], []
    for i in range(DEPTH):
        xp = xp + 0.5 * swiglu(rmsnorm(xp, ffn_norm[i, 0]), w_ffn_up[i, 0], w_ffn_down[i, 0])
        xs = xs + 0.5 * swiglu(rmsnorm(xs, ffn_norm[i, 0]), w_ffn_up[i, 0], w_ffn_down[i, 0])
        hp = rmsnorm(xp, mix_norm[i])
        hs = rmsnorm(xs, mix_norm[i])
        j = i // N_MIXERS
        if i % N_MIXERS == 0:
            wts = (mla_w_in[j], mla_q_norm[j], mla_kv_norm[j], mla_w_q_up[j], mla_w_kv_up[j], mla_w_out[j])
            op, ckv_p, kr_p = mla_prompt(hp, *wts)
            osm, ckv_s, kr_s = mla_sample(hs, cache_mla_ckv, cache_mla_krope, j, page_table, *wts)
            p_ckv.append(ckv_p)
            p_kr.append(kr_p)
            s_ckv.append(ckv_s)
            s_kr.append(kr_s)
        else:
            wts = (diff_w_in[j], diff_lambda[j], diff_subln[j], diff_w_out[j], rel_bias)
            op, k_p, v_p = diff_prompt(hp, i, *wts)
            osm, k_s, v_s = diff_sample(hs, i, cache_diff_k, cache_diff_v, j, page_table, *wts)
            p_k.append(k_p)
            p_v.append(v_p)
            s_k.append(k_s)
            s_v.append(v_s)
        xp = xp + op
        xs = xs + osm
        xp = xp + 0.5 * swiglu(rmsnorm(xp, ffn_norm[i, 1]), w_ffn_up[i, 1], w_ffn_down[i, 1])
        xs = xs + 0.5 * swiglu(rmsnorm(xs, ffn_norm[i, 1]), w_ffn_up[i, 1], w_ffn_down[i, 1])
    y_prompt = rmsnorm(xp, final_norm)
    y_sample = rmsnorm(xs, final_norm)
    return (y_prompt, y_sample,
            jnp.stack(p_ckv), jnp.stack(p_kr), jnp.stack(p_k), jnp.stack(p_v),
            jnp.stack(s_ckv), jnp.stack(s_kr), jnp.stack(s_k), jnp.stack(s_v))
```

```python
import functools
import math

import jax
import jax.numpy as jnp
import numpy as np
from jax import lax
from jax.experimental import pallas as pl
from jax.experimental.pallas import tpu as pltpu

D_MODEL = 2048
BATCH = 4
SEQ = 2048
DEPTH = 2
DEC_BATCH = 128
PAST_LEN = 16384
PAGE_SIZE = 128
N_PAGES = PAST_LEN // PAGE_SIZE
D_FF = 5632
EPS = 1e-6
MLA_HEADS = 16
Q_LORA = 512
KV_LORA = 512
QK_NOPE = 128
QK_ROPE = 64
V_HEAD = 128
MLA_SCALE = (QK_NOPE + QK_ROPE) ** -0.5
ROPE_THETA = 10000.0
DIFF_HEADS = 16
DIFF_KV_HEADS = 2
DIFF_GROUP = DIFF_HEADS // DIFF_KV_HEADS
DIFF_HEAD = 64
DIFF_VHEAD = 128
DIFF_Q = DIFF_HEADS * 2 * DIFF_HEAD
DIFF_K = DIFF_KV_HEADS * 2 * DIFF_HEAD
DIFF_V = DIFF_KV_HEADS * DIFF_VHEAD
DIFF_SCALE = DIFF_HEAD ** -0.5
NUM_BUCKETS = 32
MAX_DISTANCE = 128
NEG_INF = -1e30

LANES = 128
V7X_VMEM_LIMIT_BYTES = 60000 * 1024

FFN_ROW_TILE = 512
FFN_FF_TILE = 512
PROJ_ROW_TILE = 256
ATTN_TILE = 512
PAGES_PER_STEP = 8
KEYS_PER_STEP = PAGES_PER_STEP * PAGE_SIZE
BF16 = jnp.bfloat16
F32 = jnp.float32


def _params(sem):
    return pltpu.CompilerParams(dimension_semantics=sem, vmem_limit_bytes=V7X_VMEM_LIMIT_BYTES)


def _rms(x, g):
    return x * lax.rsqrt(jnp.mean(x * x, axis=-1, keepdims=True) + EPS) * g


def _dot(a, b):
    return jnp.dot(a, b, preferred_element_type=F32)


def _dot_nt(a, b):
    return lax.dot_general(a, b, (((1,), (1,)), ((), ())), preferred_element_type=F32)


def _resident(shape):
    nd = len(shape)
    return pl.BlockSpec(shape, lambda *_: (0,) * nd, pipeline_mode=pl.Buffered(1))


def _ffn_body(x_ref, g_ref, wg_ref, wu_ref, wd_ref, gf_ref, o_ref, h_ref, *, final):
    j = pl.program_id(1)

    @pl.when(j == 0)
    def _():
        x = x_ref[...]
        h_ref[...] = _rms(x, g_ref[...]).astype(BF16)
        o_ref[...] = x

    h = h_ref[...]
    gate = _dot(h, wg_ref[...])
    up = _dot(h, wu_ref[...])
    a = (gate / (1.0 + jnp.exp(-gate)) * up).astype(BF16)
    o_ref[...] += 0.5 * _dot(a, wd_ref[...])

    if final:
        @pl.when(j == pl.num_programs(1) - 1)
        def _():
            o_ref[...] = _rms(o_ref[...], gf_ref[...])


def _ffn(x, g, w_up, w_down, gf, *, final):
    n = x.shape[0]
    tm = min(FFN_ROW_TILE, n)
    tf = FFN_FF_TILE
    nf = D_FF // tf
    return pl.pallas_call(
        functools.partial(_ffn_body, final=final),
        grid=(n // tm, nf),
        in_specs=[
            pl.BlockSpec((tm, D_MODEL), lambda i, j: (i, 0)),
            pl.BlockSpec((1, D_MODEL), lambda i, j: (0, 0)),
            pl.BlockSpec((D_MODEL, tf), lambda i, j: (0, j)),
            pl.BlockSpec((D_MODEL, tf), lambda i, j: (0, j + nf)),
            pl.BlockSpec((tf, D_MODEL), lambda i, j: (j, 0)),
            pl.BlockSpec((1, D_MODEL), lambda i, j: (0, 0)),
        ],
        out_specs=pl.BlockSpec((tm, D_MODEL), lambda i, j: (i, 0)),
        out_shape=jax.ShapeDtypeStruct((n, D_MODEL), F32),
        scratch_shapes=[pltpu.VMEM((tm, D_MODEL), BF16)],
        compiler_params=_params(("parallel", "arbitrary")),
        name="ffn",
    )(x, g, w_up, w_up, w_down, gf)


def _outproj_body(x_ref, a_ref, w_ref, o_ref):
    o_ref[...] = x_ref[...] + _dot(a_ref[...], w_ref[...])


def _outproj(x, a, w):
    n = x.shape[0]
    tm = min(512, n)
    return pl.pallas_call(
        _outproj_body,
        grid=(n // tm,),
        in_specs=[
            pl.BlockSpec((tm, D_MODEL), lambda i: (i, 0)),
            pl.BlockSpec((tm, a.shape[1]), lambda i: (i, 0)),
            _resident(w.shape),
        ],
        out_specs=pl.BlockSpec((tm, D_MODEL), lambda i: (i, 0)),
        out_shape=jax.ShapeDtypeStruct((n, D_MODEL), F32),
        compiler_params=_params(("parallel",)),
        name="outproj",
    )(x, a, w)


def _mla_proj_body(x_ref, g_ref, win_ref, qn_ref, kvn_ref, cos_ref, sin_ref,
                   wqn_ref, wqa_ref, wqb_ref, wkv_ref,
                   ckv_ref, kr_ref, qnope_ref, qrope_ref, kv_ref, ckvb_ref, krb_ref):
    h = _rms(x_ref[...], g_ref[...]).astype(BF16)
    proj = _dot(h, win_ref[...])
    q_c = _rms(proj[:, :Q_LORA], qn_ref[...]).astype(BF16)
    ckv = _rms(proj[:, Q_LORA:Q_LORA + KV_LORA], kvn_ref[...])
    cos = cos_ref[...]
    sin = sin_ref[...]
    kr = proj[:, 1024:1152] * cos + proj[:, 1152:1280] * sin
    ckv_ref[...] = ckv
    kr_ref[...] = kr[:, :QK_ROPE]
    ckv_b = ckv.astype(BF16)
    ckvb_ref[...] = ckv_b
    krb_ref[...] = kr[:, :QK_ROPE].astype(BF16)
    qnope_ref[...] = _dot(q_c, wqn_ref[...]).astype(BF16)
    qa = _dot(q_c, wqa_ref[...])
    qb = _dot(q_c, wqb_ref[...])
    for hd in range(MLA_HEADS):
        sl = slice(hd * LANES, (hd + 1) * LANES)
        qrope_ref[:, sl] = (qa[:, sl] * cos + qb[:, sl] * sin).astype(BF16)
    kv_ref[...] = _dot(ckv_b, wkv_ref[...]).astype(BF16)


def _mla_proj(x, g, w, cos_t, sin_t):
    n = x.shape[0]
    tm = min(PROJ_ROW_TILE, n)
    tt = cos_t.shape[0]
    tb = tm if tt > 1 else 1
    nt = tt // tb
    row = lambda i: (i, 0)
    hw = MLA_HEADS * LANES
    outs = [
        jax.ShapeDtypeStruct((n, KV_LORA), F32),
        jax.ShapeDtypeStruct((n, QK_ROPE), F32),
        jax.ShapeDtypeStruct((n, hw), BF16),
        jax.ShapeDtypeStruct((n, hw), BF16),
        jax.ShapeDtypeStruct((n, 2 * hw), BF16),
        jax.ShapeDtypeStruct((n, KV_LORA), BF16),
        jax.ShapeDtypeStruct((n, QK_ROPE), BF16),
    ]
    return pl.pallas_call(
        _mla_proj_body,
        grid=(n // tm,),
        in_specs=[
            pl.BlockSpec((tm, D_MODEL), row),
            _resident((1, D_MODEL)),
            _resident(w["w_in"].shape),
            _resident((1, Q_LORA)),
            _resident((1, KV_LORA)),
            pl.BlockSpec((tb, LANES), lambda i: (i % nt, 0)),
            pl.BlockSpec((tb, LANES), lambda i: (i % nt, 0)),
            _resident(w["w_qn"].shape),
            _resident(w["w_qa"].shape),
            _resident(w["w_qb"].shape),
            _resident(w["w_kv"].shape),
        ],
        out_specs=[pl.BlockSpec((tm, s.shape[1]), row) for s in outs],
        out_shape=outs,
        compiler_params=_params(("parallel",)),
        name="mla_proj",
    )(x, g, w["w_in"], w["q_norm"], w["kv_norm"], cos_t, sin_t,
      w["w_qn"], w["w_qa"], w["w_qb"], w["w_kv"])


def _tri_schedule(nblk):
    qi, ki = [], []
    for q in range(nblk):
        for k in range(q + 1):
            qi.append(q)
            ki.append(k)
    return jnp.asarray(qi, jnp.int32), jnp.asarray(ki, jnp.int32)


def _causal_mask(s):
    r = lax.broadcasted_iota(jnp.int32, s.shape, 0)
    c = lax.broadcasted_iota(jnp.int32, s.shape, 1)
    return jnp.where(c <= r, s, NEG_INF)


def _online(s, v, m_ref, l_ref, acc_ref):
    m_old = m_ref[...]
    m_new = jnp.maximum(m_old, jnp.max(s, axis=-1, keepdims=True))
    corr = jnp.exp(m_old - m_new)
    p = jnp.exp(s - m_new)
    l_ref[...] = l_ref[...] * corr + jnp.sum(p, axis=-1, keepdims=True)
    acc_ref[...] = acc_ref[...] * corr + _dot(p.astype(BF16), v)
    m_ref[...] = m_new


def _mla_attn_body(qi_ref, ki_ref, qn_ref, qr_ref, kn_ref, kr_ref, v_ref, o_ref,
                   m_ref, l_ref, acc_ref):
    t = pl.program_id(2)
    qi = qi_ref[t]
    ki = ki_ref[t]

    @pl.when(ki == 0)
    def _():
        m_ref[...] = jnp.full(m_ref.shape, NEG_INF, F32)
        l_ref[...] = jnp.zeros(l_ref.shape, F32)
        acc_ref[...] = jnp.zeros(acc_ref.shape, F32)

    s = (_dot_nt(qn_ref[...], kn_ref[...])
         + _dot_nt(qr_ref[:, :QK_ROPE], kr_ref[...])) * MLA_SCALE

    @pl.when(ki < qi)
    def _():
        _online(s, v_ref[...], m_ref, l_ref, acc_ref)

    @pl.when(ki == qi)
    def _():
        _online(_causal_mask(s), v_ref[...], m_ref, l_ref, acc_ref)
        o_ref[...] = (acc_ref[...] / l_ref[...]).astype(BF16)


def _mla_attn(q_nope, q_rope, kv, kr_b):
    t = ATTN_TILE
    nblk = SEQ // t
    qi, ki = _tri_schedule(nblk)
    qmap = lambda b, h, s, qi, ki: (b * nblk + qi[s], h)
    grid_spec = pltpu.PrefetchScalarGridSpec(
        num_scalar_prefetch=2,
        grid=(BATCH, MLA_HEADS, qi.shape[0]),
        in_specs=[
            pl.BlockSpec((t, LANES), qmap),
            pl.BlockSpec((t, LANES), qmap),
            pl.BlockSpec((t, LANES), lambda b, h, s, qi, ki: (b * nblk + ki[s], 2 * h)),
            pl.BlockSpec((t, QK_ROPE), lambda b, h, s, qi, ki: (b * nblk + ki[s], 0)),
            pl.BlockSpec((t, LANES), lambda b, h, s, qi, ki: (b * nblk + ki[s], 2 * h + 1)),
        ],
        out_specs=pl.BlockSpec((t, LANES), qmap),
        scratch_shapes=[pltpu.VMEM((t, 1), F32), pltpu.VMEM((t, 1), F32),
                        pltpu.VMEM((t, V_HEAD), F32)],
    )
    return pl.pallas_call(
        _mla_attn_body,
        grid_spec=grid_spec,
        out_shape=jax.ShapeDtypeStruct((BATCH * SEQ, MLA_HEADS * V_HEAD), BF16),
        compiler_params=_params(("parallel", "parallel", "arbitrary")),
        name="mla_attn",
    )(qi, ki, q_nope, q_rope, kv, kr_b, kv)


def _mla_qlat_body(qn_ref, qr_ref, wukt_ref, o_ref):
    for hd in range(MLA_HEADS):
        ql = _dot(qn_ref[:, hd * LANES:(hd + 1) * LANES], wukt_ref[hd])
        o_ref[hd, :, :KV_LORA] = ql.astype(BF16)
        o_ref[hd, :, KV_LORA:] = qr_ref[:, hd * LANES:(hd + 1) * LANES]


def _mla_qlat(q_nope, q_rope, w_ukt):
    return pl.pallas_call(
        _mla_qlat_body,
        out_shape=jax.ShapeDtypeStruct((MLA_HEADS, DEC_BATCH, KV_LORA + LANES), BF16),
        compiler_params=_params(None),
        name="mla_qlat",
    )(q_nope, q_rope, w_ukt)


def _mla_decode_body(pt_ref, q_ref, cnew_ref, krnew_ref, *rest):
    ck_refs = rest[:PAGES_PER_STEP]
    kr_refs = rest[PAGES_PER_STEP:2 * PAGES_PER_STEP]
    o_ref = rest[2 * PAGES_PER_STEP]
    kc_ref, krb_ref, m_ref, l_ref, acc_ref = rest[2 * PAGES_PER_STEP + 1:]
    g = pl.program_id(1)

    @pl.when(g == 0)
    def _():
        m_ref[...] = jnp.full(m_ref.shape, NEG_INF, F32)
        l_ref[...] = jnp.zeros(l_ref.shape, F32)
        acc_ref[...] = jnp.zeros(acc_ref.shape, F32)

    for k in range(PAGES_PER_STEP):
        kc_ref[k * PAGE_SIZE:(k + 1) * PAGE_SIZE, :] = ck_refs[k][...].astype(BF16)
        krb_ref[k * PAGE_SIZE:(k + 1) * PAGE_SIZE, :] = kr_refs[k][...].astype(BF16)

    q = q_ref[...]
    q_lat = q[:, :KV_LORA]
    q_rope = q[:, KV_LORA:KV_LORA + QK_ROPE]
    s = (_dot_nt(q_lat, kc_ref[...]) + _dot_nt(q_rope, krb_ref[...])) * MLA_SCALE
    _online(s, kc_ref[...], m_ref, l_ref, acc_ref)

    @pl.when(g == pl.num_programs(1) - 1)
    def _():
        cnew = cnew_ref[...]
        s_new = (jnp.sum(q_lat.astype(F32) * cnew.astype(BF16).astype(F32), axis=-1, keepdims=True)
                 + jnp.sum(q_rope.astype(F32) * krnew_ref[...].astype(BF16).astype(F32),
                           axis=-1, keepdims=True)) * MLA_SCALE
        m_old = m_ref[...]
        m_new = jnp.maximum(m_old, s_new)
        corr = jnp.exp(m_old - m_new)
        p = jnp.exp(s_new - m_new)
        l = l_ref[...] * corr + p
        acc = acc_ref[...] * corr + p.astype(BF16).astype(F32) * cnew.astype(BF16).astype(F32)
        o_ref[...] = acc / l


def _mla_decode(page_table, q_dec, ckv_new, kr_new, cache_ckv, cache_kr):
    npg = N_PAGES // PAGES_PER_STEP

    def page_map(k):
        return lambda b, g, pt: (0, pt[b * N_PAGES + g * PAGES_PER_STEP + k], 0, 0)

    in_specs = [
        pl.BlockSpec((None, MLA_HEADS, KV_LORA + LANES), lambda b, g, pt: (b, 0, 0)),
        pl.BlockSpec((None, 1, KV_LORA), lambda b, g, pt: (b, 0, 0)),
        pl.BlockSpec((None, 1, QK_ROPE), lambda b, g, pt: (b, 0, 0)),
    ]
    in_specs += [pl.BlockSpec((None, None, PAGE_SIZE, KV_LORA), page_map(k))
                 for k in range(PAGES_PER_STEP)]
    in_specs += [pl.BlockSpec((None, None, PAGE_SIZE, QK_ROPE), page_map(k))
                 for k in range(PAGES_PER_STEP)]
    grid_spec = pltpu.PrefetchScalarGridSpec(
        num_scalar_prefetch=1,
        grid=(DEC_BATCH, npg),
        in_specs=in_specs,
        out_specs=pl.BlockSpec((None, MLA_HEADS, KV_LORA), lambda b, g, pt: (b, 0, 0)),
        scratch_shapes=[pltpu.VMEM((KEYS_PER_STEP, KV_LORA), BF16),
                        pltpu.VMEM((KEYS_PER_STEP, QK_ROPE), BF16),
                        pltpu.VMEM((MLA_HEADS, 1), F32), pltpu.VMEM((MLA_HEADS, 1), F32),
                        pltpu.VMEM((MLA_HEADS, KV_LORA), F32)],
    )
    return pl.pallas_call(
        _mla_decode_body,
        grid_spec=grid_spec,
        out_shape=jax.ShapeDtypeStruct((DEC_BATCH, MLA_HEADS, KV_LORA), F32),
        compiler_params=_params(("parallel", "arbitrary")),
        name="mla_decode",
    )(page_table.reshape(-1), q_dec, ckv_new.reshape(DEC_BATCH, 1, KV_LORA),
      kr_new.reshape(DEC_BATCH, 1, QK_ROPE),
      *([cache_ckv] * PAGES_PER_STEP), *([cache_kr] * PAGES_PER_STEP))


def _mla_finish_body(x_ref, olat_ref, wuv_ref, wout_ref, o_ref, o_scr):
    for hd in range(MLA_HEADS):
        o_scr[:, hd * V_HEAD:(hd + 1) * V_HEAD] = _dot(
            olat_ref[hd].astype(BF16), wuv_ref[hd]).astype(BF16)
    o_ref[...] = x_ref[...] + _dot(o_scr[...], wout_ref[...])


def _mla_finish(x, o_lat, w_uv, w_out):
    return pl.pallas_call(
        _mla_finish_body,
        out_shape=jax.ShapeDtypeStruct((DEC_BATCH, D_MODEL), F32),
        scratch_shapes=[pltpu.VMEM((DEC_BATCH, MLA_HEADS * V_HEAD), BF16)],
        compiler_params=_params(None),
        name="mla_finish",
    )(x, o_lat, w_uv, w_out)


def _t5_bucket(d):
    max_exact = NUM_BUCKETS // 2
    nf = jnp.maximum(d, 1).astype(F32)
    large = max_exact + (jnp.log(nf / max_exact) / math.log(MAX_DISTANCE / max_exact)
                         * (NUM_BUCKETS - max_exact)).astype(jnp.int32)
    return jnp.where(d < max_exact, d, jnp.minimum(large, NUM_BUCKETS - 1))


def _bias_lookup(bucket, rb_ref, h):
    out = jnp.zeros(bucket.shape, F32)
    for b in range(NUM_BUCKETS):
        out = jnp.where(bucket == b, rb_ref[b, h], out)
    return out


def _bias_tiles_body(rb_ref, o_ref, bk_ref):
    cls = pl.program_id(0)
    h = pl.program_id(1)

    @pl.when(h == 0)
    def _():
        r = lax.broadcasted_iota(jnp.int32, bk_ref.shape, 0)
        c = lax.broadcasted_iota(jnp.int32, bk_ref.shape, 1)
        d = jnp.maximum(cls * ATTN_TILE + r - c, 0)
        bk_ref[...] = _t5_bucket(d)

    o_ref[...] = _bias_lookup(bk_ref[...], rb_ref, h)


def _bias_tiles(rel_bias):
    t = ATTN_TILE
    return pl.pallas_call(
        _bias_tiles_body,
        grid=(3, DIFF_HEADS),
        in_specs=[pl.BlockSpec(memory_space=pltpu.SMEM)],
        out_specs=pl.BlockSpec((None, None, t, t), lambda c, h: (h, c, 0, 0)),
        out_shape=jax.ShapeDtypeStruct((DIFF_HEADS, 3, t, t), F32),
        scratch_shapes=[pltpu.VMEM((t, t), jnp.int32)],
        compiler_params=_params(("arbitrary", "arbitrary")),
        name="bias_tiles",
    )(rel_bias)


def _bias_decode_body(rb_ref, o_ref):
    c = lax.broadcasted_iota(jnp.int32, (1, 3 * LANES), 1)
    d = jnp.where(c < LANES, PAGE_SIZE - c, jnp.where(c < 2 * LANES, PAST_LEN, 0))
    bucket = _t5_bucket(d)
    for r in range(2 * DIFF_HEADS):
        o_ref[r:r + 1, :] = _bias_lookup(bucket, rb_ref, r % DIFF_HEADS)


def _bias_decode(rel_bias):
    return pl.pallas_call(
        _bias_decode_body,
        in_specs=[pl.BlockSpec(memory_space=pltpu.SMEM)],
        out_shape=jax.ShapeDtypeStruct((2 * DIFF_HEADS, 3 * LANES), F32),
        name="bias_decode",
    )(rel_bias)


def _diff_proj_body(x_ref, g_ref, w_ref, o_ref, h_ref):
    @pl.when(pl.program_id(1) == 0)
    def _():
        h_ref[...] = _rms(x_ref[...], g_ref[...]).astype(BF16)

    o_ref[...] = _dot(h_ref[...], w_ref[...])


def _diff_proj(x, g, w_in):
    n = x.shape[0]
    tm = min(512, n)
    tn = 512
    nout = w_in.shape[1]
    return pl.pallas_call(
        _diff_proj_body,
        grid=(n // tm, nout // tn),
        in_specs=[
            pl.BlockSpec((tm, D_MODEL), lambda i, j: (i, 0)),
            pl.BlockSpec((1, D_MODEL), lambda i, j: (0, 0)),
            pl.BlockSpec((D_MODEL, tn), lambda i, j: (0, j)),
        ],
        out_specs=pl.BlockSpec((tm, tn), lambda i, j: (i, j)),
        out_shape=jax.ShapeDtypeStruct((n, nout), F32),
        scratch_shapes=[pltpu.VMEM((tm, D_MODEL), BF16)],
        compiler_params=_params(("parallel", "arbitrary")),
        name="diff_proj",
    )(x, g, w_in)


def _lam_full(lam_ref, lam_init):
    lf = lam_ref[...]
    a = jnp.sum(lf[0:1] * lf[1:2], axis=-1, keepdims=True)
    b = jnp.sum(lf[2:3] * lf[3:4], axis=-1, keepdims=True)
    return jnp.exp(a) - jnp.exp(b) + lam_init


def _diff_combine(o0, o1, lam, subln, lam_init):
    a = o0 - lam * o1
    return _rms(a, subln) * (1.0 - lam_init)


def _diff_attn_body(qi_ref, ki_ref, q_ref, k_ref, v_ref, bias_ref, lam_ref, sub_ref, o_ref,
                    q0_ref, q1_ref, m_ref, l_ref, acc_ref, *, lam_init):
    t = pl.program_id(2)
    qi = qi_ref[t]
    ki = ki_ref[t]

    @pl.when(ki == 0)
    def _():
        q = q_ref[...].astype(BF16)
        lane = lax.broadcasted_iota(jnp.int32, q.shape, 1)
        zero = jnp.zeros(q.shape, BF16)
        q0_ref[...] = jnp.where(lane < DIFF_HEAD, q, zero)
        q1_ref[...] = jnp.where(lane < DIFF_HEAD, zero, q)
        m_ref[...] = jnp.full(m_ref.shape, NEG_INF, F32)
        l_ref[...] = jnp.zeros(l_ref.shape, F32)
        acc_ref[...] = jnp.zeros(acc_ref.shape, F32)

    k = k_ref[...].astype(BF16)
    v = v_ref[...].astype(BF16)
    bias = bias_ref[...]

    def update(masked):
        for mp, qm_ref in enumerate((q0_ref, q1_ref)):
            s = _dot_nt(qm_ref[...], k) * DIFF_SCALE + bias
            if masked:
                s = _causal_mask(s)
            _online(s, v, m_ref.at[mp], l_ref.at[mp], acc_ref.at[mp])

    @pl.when(ki < qi)
    def _():
        update(False)

    @pl.when(ki == qi)
    def _():
        update(True)
        lam = _lam_full(lam_ref, lam_init)
        o0 = acc_ref[0] / l_ref[0]
        o1 = acc_ref[1] / l_ref[1]
        o_ref[...] = _diff_combine(o0, o1, lam, sub_ref[...], lam_init).astype(BF16)


def _diff_attn(qkv, bias_tiles, lam, subln, lam_init):
    t = ATTN_TILE
    nblk = SEQ // t
    qi, ki = _tri_schedule(nblk)
    kcol0 = DIFF_Q // LANES
    vcol0 = (DIFF_Q + DIFF_K) // LANES
    grid_spec = pltpu.PrefetchScalarGridSpec(
        num_scalar_prefetch=2,
        grid=(BATCH, DIFF_HEADS, qi.shape[0]),
        in_specs=[
            pl.BlockSpec((t, LANES), lambda b, h, s, qi, ki: (b * nblk + qi[s], h)),
            pl.BlockSpec((t, LANES),
                         lambda b, h, s, qi, ki: (b * nblk + ki[s], kcol0 + h // DIFF_GROUP)),
            pl.BlockSpec((t, LANES),
                         lambda b, h, s, qi, ki: (b * nblk + ki[s], vcol0 + h // DIFF_GROUP)),
            pl.BlockSpec((None, None, t, t),
                         lambda b, h, s, qi, ki: (h, jnp.minimum(qi[s] - ki[s], 2), 0, 0)),
            pl.BlockSpec((4, DIFF_HEAD), lambda b, h, s, qi, ki: (0, 0)),
            pl.BlockSpec((1, DIFF_VHEAD), lambda b, h, s, qi, ki: (0, 0)),
        ],
        out_specs=pl.BlockSpec((t, LANES), lambda b, h, s, qi, ki: (b * nblk + qi[s], h)),
        scratch_shapes=[pltpu.VMEM((t, LANES), BF16), pltpu.VMEM((t, LANES), BF16),
                        pltpu.VMEM((2, t, 1), F32), pltpu.VMEM((2, t, 1), F32),
                        pltpu.VMEM((2, t, DIFF_VHEAD), F32)],
    )
    return pl.pallas_call(
        functools.partial(_diff_attn_body, lam_init=lam_init),
        grid_spec=grid_spec,
        out_shape=jax.ShapeDtypeStruct((BATCH * SEQ, DIFF_HEADS * DIFF_VHEAD), BF16),
        compiler_params=_params(("parallel", "parallel", "arbitrary")),
        name="diff_attn",
    )(qi, ki, qkv, qkv, qkv, bias_tiles, lam, subln)


def _diff_decode_body(pt_ref, q_ref, knew_ref, vnew_ref, bias_ref, lam_ref, sub_ref, *rest,
                      lam_init):
    k_refs = rest[:PAGES_PER_STEP]
    v_refs = rest[PAGES_PER_STEP:2 * PAGES_PER_STEP]
    o_ref = rest[2 * PAGES_PER_STEP]
    qbig_ref, kb_ref, vb_ref, m_ref, l_ref, acc_ref = rest[2 * PAGES_PER_STEP + 1:]
    g = pl.program_id(1)
    nrow = 2 * DIFF_HEADS
    width = DIFF_K

    def row_col_block():
        r = lax.broadcasted_iota(jnp.int32, (nrow, width), 0)
        c = lax.broadcasted_iota(jnp.int32, (nrow, width), 1)
        mp = r // DIFF_HEADS
        kvh = (r % DIFF_HEADS) // DIFF_GROUP
        return (c // DIFF_HEAD) == (kvh * 2 + mp)

    @pl.when(g == 0)
    def _():
        q = q_ref[...]
        rep = jnp.concatenate([q, q, q, q], axis=-1)
        qbig_ref[...] = jnp.where(row_col_block(), rep, 0.0).astype(BF16)
        m_ref[...] = jnp.full(m_ref.shape, NEG_INF, F32)
        l_ref[...] = jnp.zeros(l_ref.shape, F32)
        acc_ref[...] = jnp.zeros(acc_ref.shape, F32)

    for k in range(PAGES_PER_STEP):
        kb_ref[k * PAGE_SIZE:(k + 1) * PAGE_SIZE, :] = k_refs[k][...].astype(BF16)
        vb_ref[k * PAGE_SIZE:(k + 1) * PAGE_SIZE, :] = v_refs[k][...].astype(BF16)

    far = bias_ref[:, LANES:LANES + 1]
    s = _dot_nt(qbig_ref[...], kb_ref[...]) * DIFF_SCALE + far
    last = g == pl.num_programs(1) - 1

    @pl.when(jnp.logical_not(last))
    def _():
        _online(s, vb_ref[...], m_ref, l_ref, acc_ref)

    @pl.when(last)
    def _():
        col = lax.broadcasted_iota(jnp.int32, s.shape, 1)
        near = jnp.concatenate([bias_ref[:, :LANES]] * PAGES_PER_STEP, axis=-1) - far
        s2 = s + jnp.where(col >= KEYS_PER_STEP - PAGE_SIZE, near, 0.0)
        _online(s2, vb_ref[...], m_ref, l_ref, acc_ref)
        kn = knew_ref[...].astype(BF16).astype(F32)
        vn = vnew_ref[...].astype(BF16).astype(F32)
        s_new = (jnp.sum(qbig_ref[...].astype(F32) * kn, axis=-1, keepdims=True) * DIFF_SCALE
                 + bias_ref[:, 2 * LANES:2 * LANES + 1])
        m_old = m_ref[...]
        m_new = jnp.maximum(m_old, s_new)
        corr = jnp.exp(m_old - m_new)
        p = jnp.exp(s_new - m_new)
        l = l_ref[...] * corr + p
        acc = acc_ref[...] * corr + p.astype(BF16).astype(F32) * vn
        o = acc / l
        r = lax.broadcasted_iota(jnp.int32, (nrow, DIFF_VHEAD), 0)
        kvh = (r % DIFF_HEADS) // DIFF_GROUP
        osel = jnp.where(kvh == 0, o[:, :DIFF_VHEAD], o[:, DIFF_VHEAD:])
        lam = _lam_full(lam_ref, lam_init)
        o_ref[...] = _diff_combine(osel[:DIFF_HEADS], osel[DIFF_HEADS:], lam, sub_ref[...],
                                   lam_init).astype(BF16)


def _diff_decode(page_table, q32, k_new, v_new, dec_bias, lam, subln, cache_k, cache_v, lam_init):
    npg = N_PAGES // PAGES_PER_STEP

    def page_map(k):
        return lambda b, g, pt: (0, pt[b * N_PAGES + g * PAGES_PER_STEP + k], 0, 0)

    const2 = lambda b, g, pt: (0, 0)
    in_specs = [
        pl.BlockSpec((None, 2 * DIFF_HEADS, DIFF_HEAD), lambda b, g, pt: (b, 0, 0)),
        pl.BlockSpec((None, 1, DIFF_K), lambda b, g, pt: (b, 0, 0)),
        pl.BlockSpec((None, 1, DIFF_V), lambda b, g, pt: (b, 0, 0)),
        pl.BlockSpec((2 * DIFF_HEADS, 3 * LANES), const2),
        pl.BlockSpec((4, DIFF_HEAD), const2),
        pl.BlockSpec((1, DIFF_VHEAD), const2),
    ]
    in_specs += [pl.BlockSpec((None, None, PAGE_SIZE, DIFF_K), page_map(k))
                 for k in range(PAGES_PER_STEP)]
    in_specs += [pl.BlockSpec((None, None, PAGE_SIZE, DIFF_V), page_map(k))
                 for k in range(PAGES_PER_STEP)]
    grid_spec = pltpu.PrefetchScalarGridSpec(
        num_scalar_prefetch=1,
        grid=(DEC_BATCH, npg),
        in_specs=in_specs,
        out_specs=pl.BlockSpec((None, DIFF_HEADS, DIFF_VHEAD), lambda b, g, pt: (b, 0, 0)),
        scratch_shapes=[pltpu.VMEM((2 * DIFF_HEADS, DIFF_K), BF16),
                        pltpu.VMEM((KEYS_PER_STEP, DIFF_K), BF16),
                        pltpu.VMEM((KEYS_PER_STEP, DIFF_V), BF16),
                        pltpu.VMEM((2 * DIFF_HEADS, 1), F32),
                        pltpu.VMEM((2 * DIFF_HEADS, 1), F32),
                        pltpu.VMEM((2 * DIFF_HEADS, DIFF_V), F32)],
    )
    return pl.pallas_call(
        functools.partial(_diff_decode_body, lam_init=lam_init),
        grid_spec=grid_spec,
        out_shape=jax.ShapeDtypeStruct((DEC_BATCH, DIFF_HEADS, DIFF_VHEAD), BF16),
        compiler_params=_params(("parallel", "arbitrary")),
        name="diff_decode",
    )(page_table.reshape(-1), q32, k_new.reshape(DEC_BATCH, 1, DIFF_K),
      v_new.reshape(DEC_BATCH, 1, DIFF_V), dec_bias, lam, subln,
      *([cache_k] * PAGES_PER_STEP), *([cache_v] * PAGES_PER_STEP))


def _rot_half_cols(w):
    half = w.shape[-1] // 2
    return jnp.concatenate([-w[..., half:], w[..., :half]], axis=-1)


def _pad_cols(w, width):
    return jnp.pad(w, [(0, 0)] * (w.ndim - 1) + [(0, width - w.shape[-1])])


def _prep_mla(w_in, q_norm, kv_norm, w_q_up, w_kv_up, w_out):
    w_kr = w_in[:, Q_LORA + KV_LORA:]
    w_in2 = jnp.concatenate([w_in[:, :Q_LORA + KV_LORA], _pad_cols(w_kr, LANES),
                             _pad_cols(_rot_half_cols(w_kr), LANES)], axis=1)
    wq = w_q_up.reshape(Q_LORA, MLA_HEADS, QK_NOPE + QK_ROPE)
    wr = wq[:, :, QK_NOPE:]
    hw = MLA_HEADS * LANES
    wkv = w_kv_up.reshape(KV_LORA, MLA_HEADS, QK_NOPE + V_HEAD)
    return {
        "w_in": w_in2.astype(BF16),
        "q_norm": q_norm.reshape(1, Q_LORA),
        "kv_norm": kv_norm.reshape(1, KV_LORA),
        "w_qn": wq[:, :, :QK_NOPE].reshape(Q_LORA, hw).astype(BF16),
        "w_qa": _pad_cols(wr, LANES).reshape(Q_LORA, hw).astype(BF16),
        "w_qb": _pad_cols(_rot_half_cols(wr), LANES).reshape(Q_LORA, hw).astype(BF16),
        "w_kv": w_kv_up.astype(BF16),
        "w_ukt": jnp.transpose(wkv[:, :, :QK_NOPE], (1, 2, 0)).astype(BF16),
        "w_uv": jnp.transpose(wkv[:, :, QK_NOPE:], (1, 0, 2)).astype(BF16),
        "w_out": w_out.astype(BF16),
    }


def _rope_tables(pos):
    inv = ROPE_THETA ** (-jnp.arange(0, QK_ROPE, 2, dtype=F32) / QK_ROPE)
    ang = pos.astype(F32)[:, None] * inv[None, :]
    reps = LANES // (QK_ROPE // 2)
    return jnp.tile(jnp.cos(ang), (1, reps)), jnp.tile(jnp.sin(ang), (1, reps))


def kernel(x_prompt, x_sample, cache_mla_ckv, cache_mla_krope, cache_diff_k, cache_diff_v, page_table, ffn_norm, w_ffn_up, w_ffn_down, mix_norm, mla_w_in, mla_q_norm, mla_kv_norm, mla_w_q_up, mla_w_kv_up, mla_w_out, diff_w_in, diff_lambda, diff_subln, diff_w_out, rel_bias, final_norm):
    xp = x_prompt.reshape(BATCH * SEQ, D_MODEL)
    xs = x_sample.reshape(DEC_BATCH, D_MODEL)
    w_up = w_ffn_up.astype(BF16)
    w_down = w_ffn_down.astype(BF16)
    gf = final_norm.reshape(1, D_MODEL)
    cos_p, sin_p = _rope_tables(jnp.arange(SEQ, dtype=jnp.int32))
    cos_s, sin_s = _rope_tables(jnp.full((1,), PAST_LEN, jnp.int32))
    n_phys = cache_mla_ckv.shape[1]

    def ffn(x, i, k, final=False):
        return _ffn(x, ffn_norm[i, k].reshape(1, D_MODEL), w_up[i, k], w_down[i, k], gf, final=final)

    xp = ffn(xp, 0, 0)
    xs = ffn(xs, 0, 0)
    g0 = mix_norm[0].reshape(1, D_MODEL)
    wm = _prep_mla(mla_w_in[0], mla_q_norm[0], mla_kv_norm[0], mla_w_q_up[0], mla_w_kv_up[0],
                   mla_w_out[0])
    ckv_p, kr_p, qn_p, qr_p, kv_p, _, krb_p = _mla_proj(xp, g0, wm, cos_p, sin_p)
    o_p = _mla_attn(qn_p, qr_p, kv_p, krb_p)
    xp = _outproj(xp, o_p, wm["w_out"])

    ckv_s, kr_s, qn_s, qr_s, _, _, _ = _mla_proj(xs, g0, wm, cos_s, sin_s)
    q_dec = jnp.transpose(_mla_qlat(qn_s, qr_s, wm["w_ukt"]), (1, 0, 2))
    o_lat = _mla_decode(page_table, q_dec, ckv_s, kr_s, cache_mla_ckv, cache_mla_krope)
    xs = _mla_finish(xs, jnp.transpose(o_lat, (1, 0, 2)), wm["w_uv"], wm["w_out"])

    xp = ffn(xp, 0, 1)
    xs = ffn(xs, 0, 1)

    xp = ffn(xp, 1, 0)
    xs = ffn(xs, 1, 0)
    lam_init = 0.8 - 0.6 * math.exp(-0.3 * 1)
    g1 = mix_norm[1].reshape(1, D_MODEL)
    w_din = diff_w_in[0].astype(BF16)
    w_dout = diff_w_out[0].astype(BF16)
    lam = diff_lambda[0]
    subln = diff_subln[0].reshape(1, DIFF_VHEAD)
    bias_tiles = _bias_tiles(rel_bias)
    dec_bias = _bias_decode(rel_bias)

    qkv_p = _diff_proj(xp, g1, w_din)
    a_p = _diff_attn(qkv_p, bias_tiles, lam, subln, lam_init)
    xp = _outproj(xp, a_p, w_dout)

    qkv_s = _diff_proj(xs, g1, w_din)
    q32 = qkv_s[:, :DIFF_Q].reshape(DEC_BATCH, DIFF_KV_HEADS, DIFF_GROUP, 2, DIFF_HEAD)
    q32 = jnp.transpose(q32, (0, 3, 1, 2, 4)).reshape(DEC_BATCH, 2 * DIFF_HEADS, DIFF_HEAD)
    k_s = qkv_s[:, DIFF_Q:DIFF_Q + DIFF_K]
    v_s = qkv_s[:, DIFF_Q + DIFF_K:]
    a_s = _diff_decode(page_table, q32, k_s, v_s, dec_bias, lam, subln,
                       cache_diff_k.reshape(1, n_phys, PAGE_SIZE, DIFF_K),
                       cache_diff_v.reshape(1, n_phys, PAGE_SIZE, DIFF_V), lam_init)
    xs = _outproj(xs, a_s.reshape(DEC_BATCH, DIFF_HEADS * DIFF_VHEAD), w_dout)

    xp = ffn(xp, 1, 1, final=True)
    xs = ffn(xs, 1, 1, final=True)

    k_p = qkv_p[:, DIFF_Q:DIFF_Q + DIFF_K]
    v_p = qkv_p[:, DIFF_Q + DIFF_K:]
    return (
        xp.reshape(BATCH, SEQ, D_MODEL),
        xs.reshape(DEC_BATCH, 1, D_MODEL),
        ckv_p.reshape(1, BATCH, SEQ, KV_LORA),
        kr_p.reshape(1, BATCH, SEQ, QK_ROPE),
        k_p.reshape(1, BATCH, SEQ, DIFF_KV_HEADS, 2, DIFF_HEAD),
        v_p.reshape(1, BATCH, SEQ, DIFF_KV_HEADS, DIFF_VHEAD),
        ckv_s.reshape(1, DEC_BATCH, 1, KV_LORA),
        kr_s.reshape(1, DEC_BATCH, 1, QK_ROPE),
        k_s.reshape(1, DEC_BATCH, 1, DIFF_KV_HEADS, 2, DIFF_HEAD),
        v_s.reshape(1, DEC_BATCH, 1, DIFF_KV_HEADS, DIFF_VHEAD),
    )
```

```python
import functools
import math

import jax
import jax.numpy as jnp
import numpy as np
from jax import lax
from jax.experimental import pallas as pl
from jax.experimental.pallas import tpu as pltpu

D_MODEL = 2048
BATCH = 4
SEQ = 2048
DEPTH = 2
DEC_BATCH = 128
PAST_LEN = 16384
PAGE_SIZE = 128
N_PAGES = PAST_LEN // PAGE_SIZE
D_FF = 5632
EPS = 1e-6
MLA_HEADS = 16
Q_LORA = 512
KV_LORA = 512
QK_NOPE = 128
QK_ROPE = 64
V_HEAD = 128
MLA_SCALE = (QK_NOPE + QK_ROPE) ** -0.5
ROPE_THETA = 10000.0
DIFF_HEADS = 16
DIFF_KV_HEADS = 2
DIFF_GROUP = DIFF_HEADS // DIFF_KV_HEADS
DIFF_HEAD = 64
DIFF_VHEAD = 128
DIFF_Q = DIFF_HEADS * 2 * DIFF_HEAD
DIFF_K = DIFF_KV_HEADS * 2 * DIFF_HEAD
DIFF_V = DIFF_KV_HEADS * DIFF_VHEAD
DIFF_SCALE = DIFF_HEAD ** -0.5
NUM_BUCKETS = 32
MAX_DISTANCE = 128
NEG_INF = -1e30

LANES = 128
V7X_VMEM_LIMIT_BYTES = 60000 * 1024

FFN_ROW_TILE = 512
FFN_FF_TILE = 512
PROJ_ROW_TILE = 256
ATTN_TILE = 512
PAGES_PER_STEP = 16
KEYS_PER_STEP = PAGES_PER_STEP * PAGE_SIZE
PAGE_GROUPS = N_PAGES // PAGES_PER_STEP
BF16 = jnp.bfloat16
F32 = jnp.float32


def _params(sem):
    return pltpu.CompilerParams(dimension_semantics=sem, vmem_limit_bytes=V7X_VMEM_LIMIT_BYTES)


def _rms(x, g):
    return x * lax.rsqrt(jnp.mean(x * x, axis=-1, keepdims=True) + EPS) * g


def _dot(a, b):
    return jnp.dot(a, b, preferred_element_type=F32)


def _dot_nt(a, b):
    return lax.dot_general(a, b, (((1,), (1,)), ((), ())), preferred_element_type=F32)


def _resident(shape):
    nd = len(shape)
    return pl.BlockSpec(shape, lambda *_: (0,) * nd, pipeline_mode=pl.Buffered(1))


def _ffn_body(x_ref, g_ref, wg_ref, wu_ref, wd_ref, gf_ref, o_ref, h_ref, *, final):
    j = pl.program_id(1)

    @pl.when(j == 0)
    def _():
        x = x_ref[...]
        h_ref[...] = _rms(x, g_ref[...]).astype(BF16)
        o_ref[...] = x

    h = h_ref[...]
    gate = _dot(h, wg_ref[...])
    up = _dot(h, wu_ref[...])
    a = (gate / (1.0 + jnp.exp(-gate)) * up).astype(BF16)
    o_ref[...] += 0.5 * _dot(a, wd_ref[...])

    if final:
        @pl.when(j == pl.num_programs(1) - 1)
        def _():
            o_ref[...] = _rms(o_ref[...], gf_ref[...])


def _ffn(x, g, w_up, w_down, gf, *, final):
    n = x.shape[0]
    tm = min(FFN_ROW_TILE, n)
    tf = FFN_FF_TILE
    nf = D_FF // tf
    return pl.pallas_call(
        functools.partial(_ffn_body, final=final),
        grid=(n // tm, nf),
        in_specs=[
            pl.BlockSpec((tm, D_MODEL), lambda i, j: (i, 0)),
            pl.BlockSpec((1, D_MODEL), lambda i, j: (0, 0)),
            pl.BlockSpec((D_MODEL, tf), lambda i, j: (0, j)),
            pl.BlockSpec((D_MODEL, tf), lambda i, j: (0, j + nf)),
            pl.BlockSpec((tf, D_MODEL), lambda i, j: (j, 0)),
            pl.BlockSpec((1, D_MODEL), lambda i, j: (0, 0)),
        ],
        out_specs=pl.BlockSpec((tm, D_MODEL), lambda i, j: (i, 0)),
        out_shape=jax.ShapeDtypeStruct((n, D_MODEL), F32),
        scratch_shapes=[pltpu.VMEM((tm, D_MODEL), BF16)],
        compiler_params=_params(("parallel", "arbitrary")),
        name="ffn",
    )(x, g, w_up, w_up, w_down, gf)


def _outproj_body(x_ref, a_ref, w_ref, o_ref):
    o_ref[...] = x_ref[...] + _dot(a_ref[...], w_ref[...])


def _outproj(x, a, w):
    n = x.shape[0]
    tm = min(512, n)
    return pl.pallas_call(
        _outproj_body,
        grid=(n // tm,),
        in_specs=[
            pl.BlockSpec((tm, D_MODEL), lambda i: (i, 0)),
            pl.BlockSpec((tm, a.shape[1]), lambda i: (i, 0)),
            _resident(w.shape),
        ],
        out_specs=pl.BlockSpec((tm, D_MODEL), lambda i: (i, 0)),
        out_shape=jax.ShapeDtypeStruct((n, D_MODEL), F32),
        compiler_params=_params(("parallel",)),
        name="outproj",
    )(x, a, w)


def _mla_proj_body(x_ref, g_ref, win_ref, qn_ref, kvn_ref, cos_ref, sin_ref,
                   wqn_ref, wqa_ref, wqb_ref, wkv_ref,
                   ckv_ref, kr_ref, qnope_ref, qrope_ref, kv_ref, krb_ref):
    h = _rms(x_ref[...], g_ref[...]).astype(BF16)
    proj = _dot(h, win_ref[...])
    q_c = _rms(proj[:, :Q_LORA], qn_ref[...]).astype(BF16)
    ckv = _rms(proj[:, Q_LORA:Q_LORA + KV_LORA], kvn_ref[...])
    cos = cos_ref[...]
    sin = sin_ref[...]
    kr = proj[:, 1024:1152] * cos + proj[:, 1152:1280] * sin
    ckv_ref[...] = ckv
    kr_ref[...] = kr[:, :QK_ROPE]
    ckv_b = ckv.astype(BF16)
    krb_ref[...] = kr.astype(BF16)
    qnope_ref[...] = _dot(q_c, wqn_ref[...]).astype(BF16)
    qa = _dot(q_c, wqa_ref[...])
    qb = _dot(q_c, wqb_ref[...])
    for hd in range(MLA_HEADS):
        sl = slice(hd * LANES, (hd + 1) * LANES)
        qrope_ref[:, sl] = (qa[:, sl] * cos + qb[:, sl] * sin).astype(BF16)
    kv_ref[...] = _dot(ckv_b, wkv_ref[...]).astype(BF16)


def _mla_proj(x, g, w, cos_t, sin_t):
    n = x.shape[0]
    tm = min(PROJ_ROW_TILE, n)
    tt = cos_t.shape[0]
    tb = tm if tt > 1 else 1
    nt = tt // tb
    row = lambda i: (i, 0)
    hw = MLA_HEADS * LANES
    outs = [
        jax.ShapeDtypeStruct((n, KV_LORA), F32),
        jax.ShapeDtypeStruct((n, QK_ROPE), F32),
        jax.ShapeDtypeStruct((n, hw), BF16),
        jax.ShapeDtypeStruct((n, hw), BF16),
        jax.ShapeDtypeStruct((n, 2 * hw), BF16),
        jax.ShapeDtypeStruct((n, LANES), BF16),
    ]
    return pl.pallas_call(
        _mla_proj_body,
        grid=(n // tm,),
        in_specs=[
            pl.BlockSpec((tm, D_MODEL), row),
            _resident((1, D_MODEL)),
            _resident(w["w_in"].shape),
            _resident((1, Q_LORA)),
            _resident((1, KV_LORA)),
            pl.BlockSpec((tb, LANES), lambda i: (i % nt, 0)),
            pl.BlockSpec((tb, LANES), lambda i: (i % nt, 0)),
            _resident(w["w_qn"].shape),
            _resident(w["w_qa"].shape),
            _resident(w["w_qb"].shape),
            _resident(w["w_kv"].shape),
        ],
        out_specs=[pl.BlockSpec((tm, s.shape[1]), row) for s in outs],
        out_shape=outs,
        compiler_params=_params(("parallel",)),
        name="mla_proj",
    )(x, g, w["w_in"], w["q_norm"], w["kv_norm"], cos_t, sin_t,
      w["w_qn"], w["w_qa"], w["w_qb"], w["w_kv"])


def _tri_schedule(nblk):
    qi, ki = [], []
    for q in range(nblk):
        for k in range(q + 1):
            qi.append(q)
            ki.append(k)
    return jnp.asarray(qi, jnp.int32), jnp.asarray(ki, jnp.int32)


def _causal_mask(s):
    r = lax.broadcasted_iota(jnp.int32, s.shape, 0)
    c = lax.broadcasted_iota(jnp.int32, s.shape, 1)
    return jnp.where(c <= r, s, NEG_INF)


def _lane_tile(x, width):
    return jnp.concatenate([x] * (width // LANES), axis=1)


def _with_ones(v):
    return jnp.concatenate([v, jnp.ones(v.shape, BF16)], axis=1)


def _online(s, v_aug, m_ref, l_ref, acc_ref):
    m_old = m_ref[...]
    m_new = jnp.maximum(m_old, jnp.max(s, axis=-1, keepdims=True))
    corr = jnp.exp(m_old - m_new)
    p = jnp.exp(s - _lane_tile(m_new, s.shape[1]))
    pv = _dot(p.astype(BF16), v_aug)
    l_ref[...] = l_ref[...] * corr + pv[:, LANES:]
    acc_ref[...] = acc_ref[...] * corr + pv[:, :LANES]
    m_ref[...] = m_new


def _mla_attn_body(qi_ref, ki_ref, qn_ref, qr_ref, kn_ref, kr_ref, v_ref, o_ref,
                   q_ref, m_ref, l_ref, acc_ref):
    t = pl.program_id(2)
    qi = qi_ref[t]
    ki = ki_ref[t]

    @pl.when(ki == 0)
    def _():
        q_ref[...] = jnp.concatenate([qn_ref[...], qr_ref[...]], axis=1)
        m_ref[...] = jnp.full(m_ref.shape, NEG_INF, F32)
        l_ref[...] = jnp.zeros(l_ref.shape, F32)
        acc_ref[...] = jnp.zeros(acc_ref.shape, F32)

    k = jnp.concatenate([kn_ref[...], kr_ref[...]], axis=1)
    s = _dot_nt(q_ref[...], k) * MLA_SCALE
    v_aug = _with_ones(v_ref[...])

    @pl.when(ki < qi)
    def _():
        _online(s, v_aug, m_ref, l_ref, acc_ref)

    @pl.when(ki == qi)
    def _():
        _online(_causal_mask(s), v_aug, m_ref, l_ref, acc_ref)
        o_ref[...] = (acc_ref[...] / l_ref[...]).astype(BF16)


def _mla_attn(q_nope, q_rope, kv, kr_b):
    t = ATTN_TILE
    nblk = SEQ // t
    qi, ki = _tri_schedule(nblk)
    qmap = lambda b, h, s, qi, ki: (b * nblk + qi[s], h)
    grid_spec = pltpu.PrefetchScalarGridSpec(
        num_scalar_prefetch=2,
        grid=(BATCH, MLA_HEADS, qi.shape[0]),
        in_specs=[
            pl.BlockSpec((t, LANES), qmap),
            pl.BlockSpec((t, LANES), qmap),
            pl.BlockSpec((t, LANES), lambda b, h, s, qi, ki: (b * nblk + ki[s], 2 * h)),
            pl.BlockSpec((t, LANES), lambda b, h, s, qi, ki: (b * nblk + ki[s], 0)),
            pl.BlockSpec((t, LANES), lambda b, h, s, qi, ki: (b * nblk + ki[s], 2 * h + 1)),
        ],
        out_specs=pl.BlockSpec((t, LANES), qmap),
        scratch_shapes=[pltpu.VMEM((t, 2 * LANES), BF16),
                        pltpu.VMEM((t, LANES), F32), pltpu.VMEM((t, LANES), F32),
                        pltpu.VMEM((t, V_HEAD), F32)],
    )
    return pl.pallas_call(
        _mla_attn_body,
        grid_spec=grid_spec,
        out_shape=jax.ShapeDtypeStruct((BATCH * SEQ, MLA_HEADS * V_HEAD), BF16),
        compiler_params=_params(("parallel", "parallel", "arbitrary")),
        name="mla_attn",
    )(qi, ki, q_nope, q_rope, kv, kr_b, kv)


def _mla_qlat_body(qn_ref, qr_ref, wukt_ref, o_ref):
    for hd in range(MLA_HEADS):
        ql = _dot(qn_ref[:, hd * LANES:(hd + 1) * LANES], wukt_ref[hd])
        o_ref[hd, :, :KV_LORA] = ql.astype(BF16)
        o_ref[hd, :, KV_LORA:] = qr_ref[:, hd * LANES:(hd + 1) * LANES]


def _mla_qlat(q_nope, q_rope, w_ukt):
    return pl.pallas_call(
        _mla_qlat_body,
        out_shape=jax.ShapeDtypeStruct((MLA_HEADS, DEC_BATCH, KV_LORA + LANES), BF16),
        compiler_params=_params(None),
        name="mla_qlat",
    )(q_nope, q_rope, w_ukt)


def _group_copies(page_of, hbm_refs, bufs, sem, slot):
    copies = []
    for k in range(PAGES_PER_STEP):
        page = page_of(k)
        for hbm, buf in zip(hbm_refs, bufs):
            copies.append(pltpu.make_async_copy(hbm.at[0, page], buf.at[slot, k], sem.at[slot]))
    return copies


def _paged_scan(pt_ref, hbm_refs, bufs, sem, group_fn, init):
    b = pl.program_id(0)
    nb = pl.num_programs(0)
    last_base = nb * N_PAGES - PAGES_PER_STEP

    def start(base, slot):
        for c in _group_copies(lambda k: pt_ref[base + k], hbm_refs, bufs, sem, slot):
            c.start()

    def wait(slot):
        for c in _group_copies(lambda k: 0, hbm_refs, bufs, sem, slot):
            c.wait()

    @pl.when(b == 0)
    def _():
        start(0, 0)

    def group(g, slot, carry):
        start(jnp.minimum(b * N_PAGES + (g + 1) * PAGES_PER_STEP, last_base), 1 - slot)
        wait(slot)
        return group_fn(g, slot, carry)

    def pair(i, carry):
        return group(2 * i + 1, 1, group(2 * i, 0, carry))

    carry = lax.fori_loop(0, PAGE_GROUPS // 2, pair, init)

    @pl.when(b == nb - 1)
    def _():
        wait(0)

    return carry


def _softmax_step(s, m, l):
    m_new = jnp.maximum(m, jnp.max(s, axis=-1, keepdims=True))
    corr = jnp.exp(m - m_new)
    p = jnp.exp(s - _lane_tile(m_new, s.shape[1]))
    return m_new, corr, p, l * corr + jnp.sum(p, axis=-1, keepdims=True)


def _mla_decode_body(pt_ref, q_ref, cnew_ref, krnew_ref, ckv_hbm, krt_hbm, o_ref,
                     cbuf, rbuf, kc_ref, krt_ref, sem):
    q = q_ref[...]
    q_lat = q[:, :KV_LORA]
    q_rope = q[:, KV_LORA:KV_LORA + QK_ROPE]

    def group_fn(g, slot, carry):
        m, l, acc = carry
        for k in range(PAGES_PER_STEP):
            rows = slice(k * PAGE_SIZE, (k + 1) * PAGE_SIZE)
            kc_ref[rows, :] = cbuf[slot, k].astype(BF16)
            krt_ref[:, rows] = rbuf[slot, k].astype(BF16)
        kc = kc_ref[...]
        s = (_dot_nt(q_lat, kc) + _dot(q_rope, krt_ref[...])) * MLA_SCALE
        m, corr, p, l = _softmax_step(s, m, l)
        acc = acc * _lane_tile(corr, KV_LORA) + _dot(p.astype(BF16), kc)
        return m, l, acc

    init = (jnp.full((MLA_HEADS, LANES), NEG_INF, F32), jnp.zeros((MLA_HEADS, LANES), F32),
            jnp.zeros((MLA_HEADS, KV_LORA), F32))
    m, l, acc = _paged_scan(pt_ref, (ckv_hbm, krt_hbm), (cbuf, rbuf), sem, group_fn, init)

    cnew = cnew_ref[...].astype(BF16).astype(F32)
    krnew = krnew_ref[...].astype(BF16).astype(F32)
    s_new = (jnp.sum(q_lat.astype(F32) * cnew, axis=-1, keepdims=True)
             + jnp.sum(q_rope.astype(F32) * krnew, axis=-1, keepdims=True)) * MLA_SCALE
    m_new = jnp.maximum(m, s_new)
    corr = jnp.exp(m - m_new)
    p = jnp.exp(s_new - m_new)
    l = l * corr + p
    acc = acc * _lane_tile(corr, KV_LORA) + _lane_tile(p.astype(BF16).astype(F32), KV_LORA) * cnew
    o_ref[...] = acc / _lane_tile(l, KV_LORA)


def _mla_decode(page_table, q_dec, ckv_new, kr_new, cache_ckv, cache_krt):
    row = lambda b, pt: (b, 0, 0)
    grid_spec = pltpu.PrefetchScalarGridSpec(
        num_scalar_prefetch=1,
        grid=(DEC_BATCH,),
        in_specs=[
            pl.BlockSpec((None, MLA_HEADS, KV_LORA + LANES), row),
            pl.BlockSpec((None, 1, KV_LORA), row),
            pl.BlockSpec((None, 1, QK_ROPE), row),
            pl.BlockSpec(memory_space=pl.ANY),
            pl.BlockSpec(memory_space=pl.ANY),
        ],
        out_specs=pl.BlockSpec((None, MLA_HEADS, KV_LORA), row),
        scratch_shapes=[pltpu.VMEM((2, PAGES_PER_STEP, PAGE_SIZE, KV_LORA), F32),
                        pltpu.VMEM((2, PAGES_PER_STEP, QK_ROPE, PAGE_SIZE), F32),
                        pltpu.VMEM((KEYS_PER_STEP, KV_LORA), BF16),
                        pltpu.VMEM((QK_ROPE, KEYS_PER_STEP), BF16),
                        pltpu.SemaphoreType.DMA((2,))],
    )
    return pl.pallas_call(
        _mla_decode_body,
        grid_spec=grid_spec,
        out_shape=jax.ShapeDtypeStruct((DEC_BATCH, MLA_HEADS, KV_LORA), F32),
        compiler_params=_params(("arbitrary",)),
        name="mla_decode",
    )(page_table.reshape(-1), q_dec, ckv_new.reshape(DEC_BATCH, 1, KV_LORA),
      kr_new.reshape(DEC_BATCH, 1, QK_ROPE), cache_ckv, cache_krt)


def _mla_finish_body(x_ref, olat_ref, wuv_ref, wout_ref, o_ref, o_scr):
    for hd in range(MLA_HEADS):
        o_scr[:, hd * V_HEAD:(hd + 1) * V_HEAD] = _dot(
            olat_ref[hd].astype(BF16), wuv_ref[hd]).astype(BF16)
    o_ref[...] = x_ref[...] + _dot(o_scr[...], wout_ref[...])


def _mla_finish(x, o_lat, w_uv, w_out):
    return pl.pallas_call(
        _mla_finish_body,
        out_shape=jax.ShapeDtypeStruct((DEC_BATCH, D_MODEL), F32),
        scratch_shapes=[pltpu.VMEM((DEC_BATCH, MLA_HEADS * V_HEAD), BF16)],
        compiler_params=_params(None),
        name="mla_finish",
    )(x, o_lat, w_uv, w_out)


def _t5_bucket(d):
    max_exact = NUM_BUCKETS // 2
    nf = jnp.maximum(d, 1).astype(F32)
    large = max_exact + (jnp.log(nf / max_exact) / math.log(MAX_DISTANCE / max_exact)
                         * (NUM_BUCKETS - max_exact)).astype(jnp.int32)
    return jnp.where(d < max_exact, d, jnp.minimum(large, NUM_BUCKETS - 1))


def _bias_lookup(bucket, rb_ref, h):
    out = jnp.zeros(bucket.shape, F32)
    for b in range(NUM_BUCKETS):
        out = jnp.where(bucket == b, rb_ref[b, h], out)
    return out


def _bias_tiles_body(rb_ref, o_ref, bk_ref):
    cls = pl.program_id(0)
    h = pl.program_id(1)

    @pl.when(h == 0)
    def _():
        r = lax.broadcasted_iota(jnp.int32, bk_ref.shape, 0)
        c = lax.broadcasted_iota(jnp.int32, bk_ref.shape, 1)
        d = jnp.maximum(cls * ATTN_TILE + r - c, 0)
        bk_ref[...] = _t5_bucket(d)

    o_ref[...] = _bias_lookup(bk_ref[...], rb_ref, h)


def _bias_tiles(rel_bias):
    t = ATTN_TILE
    return pl.pallas_call(
        _bias_tiles_body,
        grid=(3, DIFF_HEADS),
        in_specs=[pl.BlockSpec(memory_space=pltpu.SMEM)],
        out_specs=pl.BlockSpec((None, None, t, t), lambda c, h: (h, c, 0, 0)),
        out_shape=jax.ShapeDtypeStruct((DIFF_HEADS, 3, t, t), F32),
        scratch_shapes=[pltpu.VMEM((t, t), jnp.int32)],
        compiler_params=_params(("arbitrary", "arbitrary")),
        name="bias_tiles",
    )(rel_bias)


def _bias_decode_body(rb_ref, o_ref):
    c = lax.broadcasted_iota(jnp.int32, (1, 3 * LANES), 1)
    d = jnp.where(c < LANES, PAGE_SIZE - c, jnp.where(c < 2 * LANES, PAST_LEN, 0))
    bucket = _t5_bucket(d)
    for r in range(2 * DIFF_HEADS):
        head = (r // (2 * DIFF_GROUP)) * DIFF_GROUP + r % DIFF_GROUP
        o_ref[r:r + 1, :] = _bias_lookup(bucket, rb_ref, head)


def _bias_decode(rel_bias):
    return pl.pallas_call(
        _bias_decode_body,
        in_specs=[pl.BlockSpec(memory_space=pltpu.SMEM)],
        out_shape=jax.ShapeDtypeStruct((2 * DIFF_HEADS, 3 * LANES), F32),
        name="bias_decode",
    )(rel_bias)


def _diff_proj_body(x_ref, g_ref, w_ref, o_ref, h_ref):
    @pl.when(pl.program_id(1) == 0)
    def _():
        h_ref[...] = _rms(x_ref[...], g_ref[...]).astype(BF16)

    o_ref[...] = _dot(h_ref[...], w_ref[...])


def _diff_proj(x, g, w_in):
    n = x.shape[0]
    tm = min(512, n)
    tn = 512
    nout = w_in.shape[1]
    return pl.pallas_call(
        _diff_proj_body,
        grid=(n // tm, nout // tn),
        in_specs=[
            pl.BlockSpec((tm, D_MODEL), lambda i, j: (i, 0)),
            pl.BlockSpec((1, D_MODEL), lambda i, j: (0, 0)),
            pl.BlockSpec((D_MODEL, tn), lambda i, j: (0, j)),
        ],
        out_specs=pl.BlockSpec((tm, tn), lambda i, j: (i, j)),
        out_shape=jax.ShapeDtypeStruct((n, nout), F32),
        scratch_shapes=[pltpu.VMEM((tm, D_MODEL), BF16)],
        compiler_params=_params(("parallel", "arbitrary")),
        name="diff_proj",
    )(x, g, w_in)


def _lam_full(lam_ref, lam_init):
    lf = lam_ref[...]
    a = jnp.sum(lf[0:1] * lf[1:2], axis=-1, keepdims=True)
    b = jnp.sum(lf[2:3] * lf[3:4], axis=-1, keepdims=True)
    return jnp.exp(a) - jnp.exp(b) + lam_init


def _diff_combine(o0, o1, lam, subln, lam_init):
    a = o0 - lam * o1
    return _rms(a, subln) * (1.0 - lam_init)


def _diff_attn_body(qi_ref, ki_ref, q_ref, k_ref, v_ref, bias_ref, lam_ref, sub_ref, o_ref,
                    q0_ref, q1_ref, m_ref, l_ref, acc_ref, *, lam_init):
    t = pl.program_id(2)
    qi = qi_ref[t]
    ki = ki_ref[t]

    @pl.when(ki == 0)
    def _():
        q = q_ref[...].astype(BF16)
        lane = lax.broadcasted_iota(jnp.int32, q.shape, 1)
        zero = jnp.zeros(q.shape, BF16)
        q0_ref[...] = jnp.where(lane < DIFF_HEAD, q, zero)
        q1_ref[...] = jnp.where(lane < DIFF_HEAD, zero, q)
        m_ref[...] = jnp.full(m_ref.shape, NEG_INF, F32)
        l_ref[...] = jnp.zeros(l_ref.shape, F32)
        acc_ref[...] = jnp.zeros(acc_ref.shape, F32)

    k = k_ref[...].astype(BF16)
    v_aug = _with_ones(v_ref[...].astype(BF16))
    bias = bias_ref[...]

    def update(masked):
        for mp, qm_ref in enumerate((q0_ref, q1_ref)):
            s = _dot_nt(qm_ref[...], k) * DIFF_SCALE + bias
            if masked:
                s = _causal_mask(s)
            _online(s, v_aug, m_ref.at[mp], l_ref.at[mp], acc_ref.at[mp])

    @pl.when(ki < qi)
    def _():
        update(False)

    @pl.when(ki == qi)
    def _():
        update(True)
        lam = _lam_full(lam_ref, lam_init)
        o0 = acc_ref[0] / l_ref[0]
        o1 = acc_ref[1] / l_ref[1]
        o_ref[...] = _diff_combine(o0, o1, lam, sub_ref[...], lam_init).astype(BF16)


def _diff_attn(qkv, bias_tiles, lam, subln, lam_init):
    t = ATTN_TILE
    nblk = SEQ // t
    qi, ki = _tri_schedule(nblk)
    kcol0 = DIFF_Q // LANES
    vcol0 = (DIFF_Q + DIFF_K) // LANES
    grid_spec = pltpu.PrefetchScalarGridSpec(
        num_scalar_prefetch=2,
        grid=(BATCH, DIFF_HEADS, qi.shape[0]),
        in_specs=[
            pl.BlockSpec((t, LANES), lambda b, h, s, qi, ki: (b * nblk + qi[s], h)),
            pl.BlockSpec((t, LANES),
                         lambda b, h, s, qi, ki: (b * nblk + ki[s], kcol0 + h // DIFF_GROUP)),
            pl.BlockSpec((t, LANES),
                         lambda b, h, s, qi, ki: (b * nblk + ki[s], vcol0 + h // DIFF_GROUP)),
            pl.BlockSpec((None, None, t, t),
                         lambda b, h, s, qi, ki: (h, jnp.minimum(qi[s] - ki[s], 2), 0, 0)),
            pl.BlockSpec((4, DIFF_HEAD), lambda b, h, s, qi, ki: (0, 0)),
            pl.BlockSpec((1, DIFF_VHEAD), lambda b, h, s, qi, ki: (0, 0)),
        ],
        out_specs=pl.BlockSpec((t, LANES), lambda b, h, s, qi, ki: (b * nblk + qi[s], h)),
        scratch_shapes=[pltpu.VMEM((t, LANES), BF16), pltpu.VMEM((t, LANES), BF16),
                        pltpu.VMEM((2, t, LANES), F32), pltpu.VMEM((2, t, LANES), F32),
                        pltpu.VMEM((2, t, DIFF_VHEAD), F32)],
    )
    return pl.pallas_call(
        functools.partial(_diff_attn_body, lam_init=lam_init),
        grid_spec=grid_spec,
        out_shape=jax.ShapeDtypeStruct((BATCH * SEQ, DIFF_HEADS * DIFF_VHEAD), BF16),
        compiler_params=_params(("parallel", "parallel", "arbitrary")),
        name="diff_attn",
    )(qi, ki, qkv, qkv, qkv, bias_tiles, lam, subln)


def _diff_decode_body(pt_ref, q_ref, knew_ref, vnew_ref, bias_ref, lam_ref, sub_ref,
                      kt_hbm, v_hbm, o_ref, kbuf, vbuf, ktb_ref, v0_ref, v1_ref, sem, *, lam_init):
    nrow = 2 * DIFF_HEADS
    half = nrow // 2
    r = lax.broadcasted_iota(jnp.int32, (nrow, DIFF_K), 0)
    c = lax.broadcasted_iota(jnp.int32, (nrow, DIFF_K), 1)
    q = q_ref[...]
    rep = jnp.concatenate([q, q, q, q], axis=-1)
    qbig = jnp.where((c // DIFF_HEAD) == (r // DIFF_GROUP), rep, 0.0).astype(BF16)
    far = bias_ref[:, LANES:2 * LANES]
    near_delta = bias_ref[:, :LANES] - far

    def group_fn(g, slot, carry):
        m, l, acc = carry
        for k in range(PAGES_PER_STEP):
            rows = slice(k * PAGE_SIZE, (k + 1) * PAGE_SIZE)
            ktb_ref[:, rows] = kbuf[slot, k].astype(BF16)
            v0_ref[rows, :] = vbuf[slot, k, pl.ds(0, PAGE_SIZE, stride=2), :].astype(BF16)
            v1_ref[rows, :] = vbuf[slot, k, pl.ds(1, PAGE_SIZE, stride=2), :].astype(BF16)
        s = _dot(qbig, ktb_ref[...]) * DIFF_SCALE + _lane_tile(far, KEYS_PER_STEP)
        tail = s[:, KEYS_PER_STEP - PAGE_SIZE:] + jnp.where(g == PAGE_GROUPS - 1, near_delta, 0.0)
        s = jnp.concatenate([s[:, :KEYS_PER_STEP - PAGE_SIZE], tail], axis=1)
        m, corr, p, l = _softmax_step(s, m, l)
        pb = p.astype(BF16)
        pv = jnp.concatenate([_dot(pb[:half], v0_ref[...]), _dot(pb[half:], v1_ref[...])], axis=0)
        return m, l, acc * corr + pv

    init = (jnp.full((nrow, LANES), NEG_INF, F32), jnp.zeros((nrow, LANES), F32),
            jnp.zeros((nrow, DIFF_VHEAD), F32))
    m, l, acc = _paged_scan(pt_ref, (kt_hbm, v_hbm), (kbuf, vbuf), sem, group_fn, init)

    kn = knew_ref[...].astype(BF16).astype(F32)
    vn = vnew_ref[...].astype(BF16).astype(F32)
    s_new = (jnp.sum(qbig.astype(F32) * kn, axis=-1, keepdims=True) * DIFF_SCALE
             + bias_ref[:, 2 * LANES:])
    m_new = jnp.maximum(m, s_new)
    corr = jnp.exp(m - m_new)
    p = jnp.exp(s_new - m_new)
    l = l * corr + p
    rv = lax.broadcasted_iota(jnp.int32, (nrow, DIFF_VHEAD), 0)
    v_row = jnp.where(rv < half, vn[:, :DIFF_VHEAD], vn[:, DIFF_VHEAD:])
    o = (acc * corr + p.astype(BF16).astype(F32) * v_row) / l
    g8 = DIFF_GROUP
    o0 = jnp.concatenate([o[0:g8], o[2 * g8:3 * g8]], axis=0)
    o1 = jnp.concatenate([o[g8:2 * g8], o[3 * g8:4 * g8]], axis=0)
    lam = _lam_full(lam_ref, lam_init)
    o_ref[...] = _diff_combine(o0, o1, lam, sub_ref[...], lam_init).astype(BF16)


def _diff_decode(page_table, q32, k_new, v_new, dec_bias, lam, subln, cache_kt, cache_v2, lam_init):
    row = lambda b, pt: (b, 0, 0)
    const2 = lambda b, pt: (0, 0)
    grid_spec = pltpu.PrefetchScalarGridSpec(
        num_scalar_prefetch=1,
        grid=(DEC_BATCH,),
        in_specs=[
            pl.BlockSpec((None, 2 * DIFF_HEADS, DIFF_HEAD), row),
            pl.BlockSpec((None, 1, DIFF_K), row),
            pl.BlockSpec((None, 1, DIFF_V), row),
            pl.BlockSpec((2 * DIFF_HEADS, 3 * LANES), const2),
            pl.BlockSpec((4, DIFF_HEAD), const2),
            pl.BlockSpec((1, DIFF_VHEAD), const2),
            pl.BlockSpec(memory_space=pl.ANY),
            pl.BlockSpec(memory_space=pl.ANY),
        ],
        out_specs=pl.BlockSpec((None, DIFF_HEADS, DIFF_VHEAD), row),
        scratch_shapes=[pltpu.VMEM((2, PAGES_PER_STEP, DIFF_K, PAGE_SIZE), F32),
                        pltpu.VMEM((2, PAGES_PER_STEP, 2 * PAGE_SIZE, DIFF_VHEAD), F32),
                        pltpu.VMEM((DIFF_K, KEYS_PER_STEP), BF16),
                        pltpu.VMEM((KEYS_PER_STEP, DIFF_VHEAD), BF16),
                        pltpu.VMEM((KEYS_PER_STEP, DIFF_VHEAD), BF16),
                        pltpu.SemaphoreType.DMA((2,))],
    )
    return pl.pallas_call(
        functools.partial(_diff_decode_body, lam_init=lam_init),
        grid_spec=grid_spec,
        out_shape=jax.ShapeDtypeStruct((DEC_BATCH, DIFF_HEADS, DIFF_VHEAD), BF16),
        compiler_params=_params(("arbitrary",)),
        name="diff_decode",
    )(page_table.reshape(-1), q32, k_new.reshape(DEC_BATCH, 1, DIFF_K),
      v_new.reshape(DEC_BATCH, 1, DIFF_V), dec_bias, lam, subln, cache_kt, cache_v2)


def _rot_half_cols(w):
    half = w.shape[-1] // 2
    return jnp.concatenate([-w[..., half:], w[..., :half]], axis=-1)


def _pad_cols(w, width):
    return jnp.pad(w, [(0, 0)] * (w.ndim - 1) + [(0, width - w.shape[-1])])


def _prep_mla(w_in, q_norm, kv_norm, w_q_up, w_kv_up, w_out):
    w_kr = w_in[:, Q_LORA + KV_LORA:]
    w_in2 = jnp.concatenate([w_in[:, :Q_LORA + KV_LORA], _pad_cols(w_kr, LANES),
                             _pad_cols(_rot_half_cols(w_kr), LANES)], axis=1)
    wq = w_q_up.reshape(Q_LORA, MLA_HEADS, QK_NOPE + QK_ROPE)
    wr = wq[:, :, QK_NOPE:]
    hw = MLA_HEADS * LANES
    wkv = w_kv_up.reshape(KV_LORA, MLA_HEADS, QK_NOPE + V_HEAD)
    return {
        "w_in": w_in2.astype(BF16),
        "q_norm": q_norm.reshape(1, Q_LORA),
        "kv_norm": kv_norm.reshape(1, KV_LORA),
        "w_qn": wq[:, :, :QK_NOPE].reshape(Q_LORA, hw).astype(BF16),
        "w_qa": _pad_cols(wr, LANES).reshape(Q_LORA, hw).astype(BF16),
        "w_qb": _pad_cols(_rot_half_cols(wr), LANES).reshape(Q_LORA, hw).astype(BF16),
        "w_kv": w_kv_up.astype(BF16),
        "w_ukt": jnp.transpose(wkv[:, :, :QK_NOPE], (1, 2, 0)).astype(BF16),
        "w_uv": jnp.transpose(wkv[:, :, QK_NOPE:], (1, 0, 2)).astype(BF16),
        "w_out": w_out.astype(BF16),
    }


def _rope_tables(pos):
    inv = ROPE_THETA ** (-jnp.arange(0, QK_ROPE, 2, dtype=F32) / QK_ROPE)
    ang = pos.astype(F32)[:, None] * inv[None, :]
    reps = LANES // (QK_ROPE // 2)
    return jnp.tile(jnp.cos(ang), (1, reps)), jnp.tile(jnp.sin(ang), (1, reps))


def kernel(x_prompt, x_sample, cache_mla_ckv, cache_mla_krope, cache_diff_k, cache_diff_v, page_table, ffn_norm, w_ffn_up, w_ffn_down, mix_norm, mla_w_in, mla_q_norm, mla_kv_norm, mla_w_q_up, mla_w_kv_up, mla_w_out, diff_w_in, diff_lambda, diff_subln, diff_w_out, rel_bias, final_norm):
    xp = x_prompt.reshape(BATCH * SEQ, D_MODEL)
    xs = x_sample.reshape(DEC_BATCH, D_MODEL)
    w_up = w_ffn_up.astype(BF16)
    w_down = w_ffn_down.astype(BF16)
    gf = final_norm.reshape(1, D_MODEL)
    cos_p, sin_p = _rope_tables(jnp.arange(SEQ, dtype=jnp.int32))
    cos_s, sin_s = _rope_tables(jnp.full((1,), PAST_LEN, jnp.int32))
    n_phys = cache_mla_ckv.shape[1]

    def ffn(x, i, k, final=False):
        return _ffn(x, ffn_norm[i, k].reshape(1, D_MODEL), w_up[i, k], w_down[i, k], gf, final=final)

    xp = ffn(xp, 0, 0)
    xs = ffn(xs, 0, 0)
    g0 = mix_norm[0].reshape(1, D_MODEL)
    wm = _prep_mla(mla_w_in[0], mla_q_norm[0], mla_kv_norm[0], mla_w_q_up[0], mla_w_kv_up[0],
                   mla_w_out[0])
    ckv_p, kr_p, qn_p, qr_p, kv_p, krb_p = _mla_proj(xp, g0, wm, cos_p, sin_p)
    o_p = _mla_attn(qn_p, qr_p, kv_p, krb_p)
    xp = _outproj(xp, o_p, wm["w_out"])

    ckv_s, kr_s, qn_s, qr_s, _, _ = _mla_proj(xs, g0, wm, cos_s, sin_s)
    q_dec = jnp.transpose(_mla_qlat(qn_s, qr_s, wm["w_ukt"]), (1, 0, 2))
    cache_krt = jnp.transpose(cache_mla_krope, (0, 1, 3, 2))
    cache_kt = jnp.transpose(cache_diff_k, (0, 1, 3, 4, 5, 2)).reshape(1, n_phys, DIFF_K, PAGE_SIZE)
    cache_v2 = cache_diff_v.reshape(1, n_phys, 2 * PAGE_SIZE, DIFF_VHEAD)
    o_lat = _mla_decode(page_table, q_dec, ckv_s, kr_s, cache_mla_ckv, cache_krt)
    xs = _mla_finish(xs, jnp.transpose(o_lat, (1, 0, 2)), wm["w_uv"], wm["w_out"])

    xp = ffn(xp, 0, 1)
    xs = ffn(xs, 0, 1)

    xp = ffn(xp, 1, 0)
    xs = ffn(xs, 1, 0)
    lam_init = 0.8 - 0.6 * math.exp(-0.3 * 1)
    g1 = mix_norm[1].reshape(1, D_MODEL)
    w_din = diff_w_in[0].astype(BF16)
    w_dout = diff_w_out[0].astype(BF16)
    lam = diff_lambda[0]
    subln = diff_subln[0].reshape(1, DIFF_VHEAD)
    bias_tiles = _bias_tiles(rel_bias)
    dec_bias = _bias_decode(rel_bias)

    qkv_p = _diff_proj(xp, g1, w_din)
    a_p = _diff_attn(qkv_p, bias_tiles, lam, subln, lam_init)
    xp = _outproj(xp, a_p, w_dout)

    qkv_s = _diff_proj(xs, g1, w_din)
    q32 = qkv_s[:, :DIFF_Q].reshape(DEC_BATCH, DIFF_KV_HEADS, DIFF_GROUP, 2, DIFF_HEAD)
    q32 = jnp.transpose(q32, (0, 1, 3, 2, 4)).reshape(DEC_BATCH, 2 * DIFF_HEADS, DIFF_HEAD)
    k_s = qkv_s[:, DIFF_Q:DIFF_Q + DIFF_K]
    v_s = qkv_s[:, DIFF_Q + DIFF_K:]
    a_s = _diff_decode(page_table, q32, k_s, v_s, dec_bias, lam, subln, cache_kt, cache_v2,
                       lam_init)
    xs = _outproj(xs, a_s.reshape(DEC_BATCH, DIFF_HEADS * DIFF_VHEAD), w_dout)

    xp = ffn(xp, 1, 1, final=True)
    xs = ffn(xs, 1, 1, final=True)

    k_p = qkv_p[:, DIFF_Q:DIFF_Q + DIFF_K]
    v_p = qkv_p[:, DIFF_Q + DIFF_K:]
    return (
        xp.reshape(BATCH, SEQ, D_MODEL),
        xs.reshape(DEC_BATCH, 1, D_MODEL),
        ckv_p.reshape(1, BATCH, SEQ, KV_LORA),
        kr_p.reshape(1, BATCH, SEQ, QK_ROPE),
        k_p.reshape(1, BATCH, SEQ, DIFF_KV_HEADS, 2, DIFF_HEAD),
        v_p.reshape(1, BATCH, SEQ, DIFF_KV_HEADS, DIFF_VHEAD),
        ckv_s.reshape(1, DEC_BATCH, 1, KV_LORA),
        kr_s.reshape(1, DEC_BATCH, 1, QK_ROPE),
        k_s.reshape(1, DEC_BATCH, 1, DIFF_KV_HEADS, 2, DIFF_HEAD),
        v_s.reshape(1, DEC_BATCH, 1, DIFF_KV_HEADS, DIFF_VHEAD),
    )
```

```python
import functools
import math
from typing import Any, Callable, NamedTuple

import jax
import jax.numpy as jnp
import numpy as np
from jax import lax
from jax.experimental import pallas as pl
from jax.experimental.pallas import tpu as pltpu

D_MODEL = 2048
BATCH = 4
SEQ = 2048
DEPTH = 2
DEC_BATCH = 128
PAST_LEN = 16384
PAGE_SIZE = 128
N_PAGES = PAST_LEN // PAGE_SIZE
D_FF = 5632
EPS = 1e-6
MLA_HEADS = 16
Q_LORA = 512
KV_LORA = 512
QK_NOPE = 128
QK_ROPE = 64
V_HEAD = 128
MLA_SCALE = (QK_NOPE + QK_ROPE) ** -0.5
ROPE_THETA = 10000.0
DIFF_HEADS = 16
DIFF_KV_HEADS = 2
DIFF_GROUP = DIFF_HEADS // DIFF_KV_HEADS
DIFF_HEAD = 64
DIFF_VHEAD = 128
DIFF_Q = DIFF_HEADS * 2 * DIFF_HEAD
DIFF_K = DIFF_KV_HEADS * 2 * DIFF_HEAD
DIFF_V = DIFF_KV_HEADS * DIFF_VHEAD
DIFF_SCALE = DIFF_HEAD ** -0.5
NUM_BUCKETS = 32
MAX_DISTANCE = 128
NEG_INF = -1e30

LANES = 128
V7X_VMEM_LIMIT_BYTES = 60000 * 1024

FFN_ROW_TILE = 512
FFN_FF_TILE = 512
PROJ_ROW_TILE = 256
ATTN_TILE = 512
PAGES_PER_STEP = 16
KEYS_PER_STEP = PAGES_PER_STEP * PAGE_SIZE
PAGE_GROUPS = N_PAGES // PAGES_PER_STEP
UNITS_PER_STEP = 4
STEPS_PER_ROW = PAGE_GROUPS // UNITS_PER_STEP
assert math.frexp(DIFF_SCALE)[0] == 0.5
BF16 = jnp.bfloat16
F32 = jnp.float32


def _params(sem):
    return pltpu.CompilerParams(dimension_semantics=sem, vmem_limit_bytes=V7X_VMEM_LIMIT_BYTES)


def _rms(x, g):
    return x * lax.rsqrt(jnp.mean(x * x, axis=-1, keepdims=True) + EPS) * g


def _dot(a, b):
    return jnp.dot(a, b, preferred_element_type=F32)


def _dot_nt(a, b):
    return lax.dot_general(a, b, (((1,), (1,)), ((), ())), preferred_element_type=F32)


def _resident(shape):
    nd = len(shape)
    return pl.BlockSpec(shape, lambda *_: (0,) * nd, pipeline_mode=pl.Buffered(1))


class _Side(NamedTuple):
    page_table: Any
    inputs: tuple
    in_specs: tuple
    out_shape: Any
    scratch: tuple
    begin: Callable
    run: Callable


def _ffn_body(*refs, final, side):
    n_side_in = len(side.inputs) if side else 0
    pt_ref = refs[0] if side else None
    refs = refs[1:] if side else refs
    x_ref, g_ref, wg_ref, wu_ref, wd_ref, gf_ref = refs[:6]
    side_in = refs[6:6 + n_side_in]
    o_ref = refs[6 + n_side_in]
    rest = refs[7 + n_side_in:]
    side_out, rest = (rest[0], rest[1:]) if side else (None, rest)
    h_ref, side_scratch = rest[0], rest[1:]
    j = pl.program_id(1)
    step = pl.program_id(0) * pl.num_programs(1) + j

    if side:
        side.begin(step, pt_ref, side_in, side_scratch)

    @pl.when(j == 0)
    def _():
        x = x_ref[...]
        h_ref[...] = _rms(x, g_ref[...]).astype(BF16)
        o_ref[...] = x

    h = h_ref[...]
    gate = _dot(h, wg_ref[...])
    up = _dot(h, wu_ref[...])
    a = (gate / (1.0 + jnp.exp(-gate)) * up).astype(BF16)
    o_ref[...] += 0.5 * _dot(a, wd_ref[...])

    if final:
        @pl.when(j == pl.num_programs(1) - 1)
        def _():
            o_ref[...] = _rms(o_ref[...], gf_ref[...])

    if side:
        side.run(step, pt_ref, side_in, side_out, side_scratch)


def _ffn(x, g, w_up, w_down, gf, layer, pos, *, final, side=None):
    n = x.shape[0]
    tm = min(FFN_ROW_TILE, n)
    tf = FFN_FF_TILE
    nf = D_FF // tf
    in_specs = [
        pl.BlockSpec((tm, D_MODEL), lambda i, j, *_: (i, 0)),
        pl.BlockSpec((1, D_MODEL), lambda i, j, *_: (0, 0)),
        pl.BlockSpec((None, None, D_MODEL, tf), lambda i, j, *_: (layer, pos, 0, j)),
        pl.BlockSpec((None, None, D_MODEL, tf), lambda i, j, *_: (layer, pos, 0, j + nf)),
        pl.BlockSpec((None, None, tf, D_MODEL), lambda i, j, *_: (layer, pos, j, 0)),
        pl.BlockSpec((1, D_MODEL), lambda i, j, *_: (0, 0)),
    ]
    out_specs = pl.BlockSpec((tm, D_MODEL), lambda i, j, *_: (i, 0))
    out_shape = jax.ShapeDtypeStruct((n, D_MODEL), F32)
    scratch = [pltpu.VMEM((tm, D_MODEL), BF16)]
    operands = (x, g, w_up, w_up, w_down, gf)
    if side:
        assert (n // tm) * nf >= side.out_shape.shape[0] * PAGE_GROUPS // UNITS_PER_STEP
        in_specs += list(side.in_specs)
        nd_side = len(side.out_shape.shape)
        out_specs = [out_specs, pl.BlockSpec(side.out_shape.shape, lambda *_: (0,) * nd_side)]
        out_shape = [out_shape, side.out_shape]
        scratch += list(side.scratch)
        operands = (side.page_table,) + operands + tuple(side.inputs)
    grid_spec = pltpu.PrefetchScalarGridSpec(
        num_scalar_prefetch=1 if side else 0, grid=(n // tm, nf),
        in_specs=in_specs, out_specs=out_specs, scratch_shapes=scratch)
    sem = ("arbitrary", "arbitrary") if side else ("parallel", "arbitrary")
    return pl.pallas_call(
        functools.partial(_ffn_body, final=final, side=side),
        grid_spec=grid_spec,
        out_shape=out_shape,
        compiler_params=_params(sem),
        name="ffn_decode" if side else "ffn",
    )(*operands)


def _outproj_body(x_ref, a_ref, w_ref, o_ref):
    o_ref[...] = x_ref[...] + _dot(a_ref[...], w_ref[...])


def _outproj(x, a, w):
    n = x.shape[0]
    tm = min(512, n)
    return pl.pallas_call(
        _outproj_body,
        grid=(n // tm,),
        in_specs=[
            pl.BlockSpec((tm, D_MODEL), lambda i: (i, 0)),
            pl.BlockSpec((tm, a.shape[1]), lambda i: (i, 0)),
            _resident(w.shape),
        ],
        out_specs=pl.BlockSpec((tm, D_MODEL), lambda i: (i, 0)),
        out_shape=jax.ShapeDtypeStruct((n, D_MODEL), F32),
        compiler_params=_params(("parallel",)),
        name="outproj",
    )(x, a, w)


def _mla_proj_body(x_ref, g_ref, win_ref, qn_ref, kvn_ref, cos_ref, sin_ref,
                   wqn_ref, wqa_ref, wqb_ref, wkv_ref,
                   ckv_ref, kr_ref, qnope_ref, qrope_ref, kv_ref, krb_ref):
    h = _rms(x_ref[...], g_ref[...]).astype(BF16)
    proj = _dot(h, win_ref[...])
    q_c = _rms(proj[:, :Q_LORA], qn_ref[...]).astype(BF16)
    ckv = _rms(proj[:, Q_LORA:Q_LORA + KV_LORA], kvn_ref[...])
    cos = cos_ref[...]
    sin = sin_ref[...]
    kr = proj[:, 1024:1152] * cos + proj[:, 1152:1280] * sin
    ckv_ref[...] = ckv
    kr_ref[...] = kr[:, :QK_ROPE]
    ckv_b = ckv.astype(BF16)
    krb_ref[...] = kr.astype(BF16)
    qnope_ref[...] = _dot(q_c, wqn_ref[...]).astype(BF16)
    qa = _dot(q_c, wqa_ref[...])
    qb = _dot(q_c, wqb_ref[...])
    for hd in range(MLA_HEADS):
        sl = slice(hd * LANES, (hd + 1) * LANES)
        qrope_ref[:, sl] = (qa[:, sl] * cos + qb[:, sl] * sin).astype(BF16)
    kv_ref[...] = _dot(ckv_b, wkv_ref[...]).astype(BF16)


def _mla_proj(x, g, w, cos_t, sin_t):
    n = x.shape[0]
    tm = min(PROJ_ROW_TILE, n)
    tt = cos_t.shape[0]
    tb = tm if tt > 1 else 1
    nt = tt // tb
    row = lambda i: (i, 0)
    hw = MLA_HEADS * LANES
    outs = [
        jax.ShapeDtypeStruct((n, KV_LORA), F32),
        jax.ShapeDtypeStruct((n, QK_ROPE), F32),
        jax.ShapeDtypeStruct((n, hw), BF16),
        jax.ShapeDtypeStruct((n, hw), BF16),
        jax.ShapeDtypeStruct((n, 2 * hw), BF16),
        jax.ShapeDtypeStruct((n, LANES), BF16),
    ]
    return pl.pallas_call(
        _mla_proj_body,
        grid=(n // tm,),
        in_specs=[
            pl.BlockSpec((tm, D_MODEL), row),
            _resident((1, D_MODEL)),
            _resident(w["w_in"].shape),
            _resident((1, Q_LORA)),
            _resident((1, KV_LORA)),
            pl.BlockSpec((tb, LANES), lambda i: (i % nt, 0)),
            pl.BlockSpec((tb, LANES), lambda i: (i % nt, 0)),
            _resident(w["w_qn"].shape),
            _resident(w["w_qa"].shape),
            _resident(w["w_qb"].shape),
            _resident(w["w_kv"].shape),
        ],
        out_specs=[pl.BlockSpec((tm, s.shape[1]), row) for s in outs],
        out_shape=outs,
        compiler_params=_params(("parallel",)),
        name="mla_proj",
    )(x, g, w["w_in"], w["q_norm"], w["kv_norm"], cos_t, sin_t,
      w["w_qn"], w["w_qa"], w["w_qb"], w["w_kv"])


def _tri_schedule(nblk):
    qi, ki = [], []
    for q in range(nblk):
        for k in range(q + 1):
            qi.append(q)
            ki.append(k)
    return jnp.asarray(qi, jnp.int32), jnp.asarray(ki, jnp.int32)


def _causal_mask(s):
    r = lax.broadcasted_iota(jnp.int32, s.shape, 0)
    c = lax.broadcasted_iota(jnp.int32, s.shape, 1)
    return jnp.where(c <= r, s, NEG_INF)


def _lane_tile(x, width):
    return jnp.concatenate([x] * (width // LANES), axis=1)


def _with_ones(v):
    return jnp.concatenate([v, jnp.ones(v.shape, BF16)], axis=1)


def _online(s, v_aug, m_ref, l_ref, acc_ref):
    m_old = m_ref[...]
    m_new = jnp.maximum(m_old, jnp.max(s, axis=-1, keepdims=True))
    corr = jnp.exp(m_old - m_new)
    p = jnp.exp(s - _lane_tile(m_new, s.shape[1]))
    pv = _dot(p.astype(BF16), v_aug)
    l_ref[...] = l_ref[...] * corr + pv[:, LANES:]
    acc_ref[...] = acc_ref[...] * corr + pv[:, :LANES]
    m_ref[...] = m_new


def _mla_attn_body(qi_ref, ki_ref, qn_ref, qr_ref, kn_ref, kr_ref, v_ref, o_ref,
                   q_ref, m_ref, l_ref, acc_ref):
    t = pl.program_id(2)
    qi = qi_ref[t]
    ki = ki_ref[t]

    @pl.when(ki == 0)
    def _():
        q_ref[...] = jnp.concatenate([qn_ref[...], qr_ref[...]], axis=1)
        m_ref[...] = jnp.full(m_ref.shape, NEG_INF, F32)
        l_ref[...] = jnp.zeros(l_ref.shape, F32)
        acc_ref[...] = jnp.zeros(acc_ref.shape, F32)

    k = jnp.concatenate([kn_ref[...], kr_ref[...]], axis=1)
    s = _dot_nt(q_ref[...], k) * MLA_SCALE
    v_aug = _with_ones(v_ref[...])

    @pl.when(ki < qi)
    def _():
        _online(s, v_aug, m_ref, l_ref, acc_ref)

    @pl.when(ki == qi)
    def _():
        _online(_causal_mask(s), v_aug, m_ref, l_ref, acc_ref)
        o_ref[...] = (acc_ref[...] / l_ref[...]).astype(BF16)


def _mla_attn(q_nope, q_rope, kv, kr_b):
    t = ATTN_TILE
    nblk = SEQ // t
    qi, ki = _tri_schedule(nblk)
    qmap = lambda b, h, s, qi, ki: (b * nblk + qi[s], h)
    grid_spec = pltpu.PrefetchScalarGridSpec(
        num_scalar_prefetch=2,
        grid=(BATCH, MLA_HEADS, qi.shape[0]),
        in_specs=[
            pl.BlockSpec((t, LANES), qmap),
            pl.BlockSpec((t, LANES), qmap),
            pl.BlockSpec((t, LANES), lambda b, h, s, qi, ki: (b * nblk + ki[s], 2 * h)),
            pl.BlockSpec((t, LANES), lambda b, h, s, qi, ki: (b * nblk + ki[s], 0)),
            pl.BlockSpec((t, LANES), lambda b, h, s, qi, ki: (b * nblk + ki[s], 2 * h + 1)),
        ],
        out_specs=pl.BlockSpec((t, LANES), qmap),
        scratch_shapes=[pltpu.VMEM((t, 2 * LANES), BF16),
                        pltpu.VMEM((t, LANES), F32), pltpu.VMEM((t, LANES), F32),
                        pltpu.VMEM((t, V_HEAD), F32)],
    )
    return pl.pallas_call(
        _mla_attn_body,
        grid_spec=grid_spec,
        out_shape=jax.ShapeDtypeStruct((BATCH * SEQ, MLA_HEADS * V_HEAD), BF16),
        compiler_params=_params(("parallel", "parallel", "arbitrary")),
        name="mla_attn",
    )(qi, ki, q_nope, q_rope, kv, kr_b, kv)


def _mla_qlat_body(qn_ref, qr_ref, wukt_ref, o_ref):
    for hd in range(MLA_HEADS):
        ql = _dot(qn_ref[:, hd * LANES:(hd + 1) * LANES], wukt_ref[hd])
        o_ref[hd, :, :KV_LORA] = ql.astype(BF16)
        o_ref[hd, :, KV_LORA:] = qr_ref[:, hd * LANES:(hd + 1) * LANES]


def _mla_qlat(q_nope, q_rope, w_ukt):
    return pl.pallas_call(
        _mla_qlat_body,
        out_shape=jax.ShapeDtypeStruct((MLA_HEADS, DEC_BATCH, KV_LORA + LANES), BF16),
        compiler_params=_params(None),
        name="mla_qlat",
    )(q_nope, q_rope, w_ukt)


def _group_copies(page_of, hbm_refs, bufs, sem, slot):
    copies = []
    for k in range(PAGES_PER_STEP):
        page = page_of(k)
        for hbm, buf in zip(hbm_refs, bufs):
            copies.append(pltpu.make_async_copy(hbm.at[0, page], buf.at[slot, k], sem.at[slot]))
    return copies


def _start_group(pt_ref, base, hbm_refs, bufs, sem, slot):
    copies = _group_copies(lambda k: pt_ref[base + k], hbm_refs, bufs, sem, slot)
    for i, c in enumerate(copies):
        c.start(priority=(i // len(hbm_refs)) % 2)


def _wait_group(hbm_refs, bufs, sem, slot):
    for c in _group_copies(lambda k: 0, hbm_refs, bufs, sem, slot):
        c.wait()


def _side_begin(step, pt_ref, hbm_refs, bufs, sem, row0):
    @pl.when(step == 0)
    def _():
        _start_group(pt_ref, row0 * N_PAGES, hbm_refs, bufs, sem, 0)


def _side_units(step, pt_ref, hbm_refs, bufs, sem, row0, n_rows, unit_fn):
    n_steps = n_rows * STEPS_PER_ROW
    first_base = row0 * N_PAGES
    last_base = (row0 + n_rows) * N_PAGES - PAGES_PER_STEP

    @pl.when(step < n_steps)
    def _():
        for i in range(UNITS_PER_STEP):
            slot = i % 2
            nxt = first_base + (step * UNITS_PER_STEP + i + 1) * PAGES_PER_STEP
            _start_group(pt_ref, jnp.minimum(nxt, last_base), hbm_refs, bufs, sem, 1 - slot)
            _wait_group(hbm_refs, bufs, sem, slot)
            unit_fn(step // STEPS_PER_ROW, step % STEPS_PER_ROW, i, slot)

    @pl.when(step == n_steps - 1)
    def _():
        _wait_group(hbm_refs, bufs, sem, 0)


def _when_first_group(sub, i, fn):
    if i == 0:
        pl.when(sub == 0)(fn)


def _when_last_group(sub, i, fn):
    if i == UNITS_PER_STEP - 1:
        pl.when(sub == STEPS_PER_ROW - 1)(fn)


def _softmax_step(s, m, l):
    m_new = jnp.maximum(m, jnp.max(s, axis=-1, keepdims=True))
    corr = jnp.exp(m - m_new)
    p = jnp.exp(s - _lane_tile(m_new, s.shape[1]))
    return m_new, corr, p, l * corr + jnp.sum(p, axis=-1, keepdims=True)


def _mla_side(page_table, row0, n_rows, q_dec, ckv_new, kr_new, cache_ckv, cache_krt):
    rows = slice(row0, row0 + n_rows)
    inputs = (q_dec[rows], ckv_new[rows].reshape(n_rows, 1, KV_LORA),
              kr_new[rows].reshape(n_rows, 1, QK_ROPE), cache_ckv, cache_krt)
    in_specs = (_resident(inputs[0].shape), _resident(inputs[1].shape), _resident(inputs[2].shape),
                pl.BlockSpec(memory_space=pl.ANY), pl.BlockSpec(memory_space=pl.ANY))
    scratch = (pltpu.VMEM((2, PAGES_PER_STEP, PAGE_SIZE, KV_LORA), F32),
               pltpu.VMEM((2, PAGES_PER_STEP, QK_ROPE, PAGE_SIZE), F32),
               pltpu.VMEM((KEYS_PER_STEP, KV_LORA), BF16),
               pltpu.VMEM((QK_ROPE, KEYS_PER_STEP), BF16),
               pltpu.VMEM((MLA_HEADS, LANES), F32), pltpu.VMEM((MLA_HEADS, LANES), F32),
               pltpu.VMEM((MLA_HEADS, KV_LORA), F32),
               pltpu.SemaphoreType.DMA((2,)))

    def begin(step, pt_ref, in_refs, scratch_refs):
        cbuf, rbuf = scratch_refs[:2]
        _side_begin(step, pt_ref, in_refs[3:5], (cbuf, rbuf), scratch_refs[-1], row0)

    def run(step, pt_ref, in_refs, o_ref, scratch_refs):
        q_ref, cnew_ref, krnew_ref, ckv_hbm, krt_hbm = in_refs
        cbuf, rbuf, kc_ref, krt_ref, m_ref, l_ref, acc_ref, sem = scratch_refs

        def unit(row, sub, i, slot):
            def init():
                m_ref[...] = jnp.full(m_ref.shape, NEG_INF, F32)
                l_ref[...] = jnp.zeros(l_ref.shape, F32)
                acc_ref[...] = jnp.zeros(acc_ref.shape, F32)

            _when_first_group(sub, i, init)
            q = q_ref[row]
            q_lat = q[:, :KV_LORA]
            q_rope = q[:, KV_LORA:KV_LORA + QK_ROPE]
            for k in range(PAGES_PER_STEP):
                keys = slice(k * PAGE_SIZE, (k + 1) * PAGE_SIZE)
                kc_ref[keys, :] = cbuf[slot, k].astype(BF16)
                krt_ref[:, keys] = rbuf[slot, k].astype(BF16)
            kc = kc_ref[...]
            s = (_dot_nt(q_lat, kc) + _dot(q_rope, krt_ref[...])) * MLA_SCALE
            m, corr, p, l = _softmax_step(s, m_ref[...], l_ref[...])
            acc = acc_ref[...] * _lane_tile(corr, KV_LORA) + _dot(p.astype(BF16), kc)
            m_ref[...] = m
            l_ref[...] = l
            acc_ref[...] = acc

            def finish():
                cnew = cnew_ref[row].astype(BF16).astype(F32)
                krnew = krnew_ref[row].astype(BF16).astype(F32)
                s_new = (jnp.sum(q_lat.astype(F32) * cnew, axis=-1, keepdims=True)
                         + jnp.sum(q_rope.astype(F32) * krnew, axis=-1, keepdims=True)) * MLA_SCALE
                m_new = jnp.maximum(m, s_new)
                c2 = jnp.exp(m - m_new)
                p_new = jnp.exp(s_new - m_new)
                l2 = l * c2 + p_new
                acc2 = (acc * _lane_tile(c2, KV_LORA)
                        + _lane_tile(p_new.astype(BF16).astype(F32), KV_LORA) * cnew)
                o_ref[row] = acc2 / _lane_tile(l2, KV_LORA)

            _when_last_group(sub, i, finish)

        _side_units(step, pt_ref, (ckv_hbm, krt_hbm), (cbuf, rbuf), sem, row0, n_rows, unit)

    return _Side(page_table.reshape(-1), inputs, in_specs,
                 jax.ShapeDtypeStruct((n_rows, MLA_HEADS, KV_LORA), F32), scratch, begin, run)


def _mla_finish_body(x_ref, olat_ref, wuv_ref, wout_ref, o_ref, o_scr):
    for hd in range(MLA_HEADS):
        o_scr[:, hd * V_HEAD:(hd + 1) * V_HEAD] = _dot(
            olat_ref[hd].astype(BF16), wuv_ref[hd]).astype(BF16)
    o_ref[...] = x_ref[...] + _dot(o_scr[...], wout_ref[...])


def _mla_finish(x, o_lat, w_uv, w_out):
    return pl.pallas_call(
        _mla_finish_body,
        out_shape=jax.ShapeDtypeStruct((DEC_BATCH, D_MODEL), F32),
        scratch_shapes=[pltpu.VMEM((DEC_BATCH, MLA_HEADS * V_HEAD), BF16)],
        compiler_params=_params(None),
        name="mla_finish",
    )(x, o_lat, w_uv, w_out)


def _t5_bucket(d):
    max_exact = NUM_BUCKETS // 2
    nf = jnp.maximum(d, 1).astype(F32)
    large = max_exact + (jnp.log(nf / max_exact) / math.log(MAX_DISTANCE / max_exact)
                         * (NUM_BUCKETS - max_exact)).astype(jnp.int32)
    return jnp.where(d < max_exact, d, jnp.minimum(large, NUM_BUCKETS - 1))


def _bias_lookup(bucket, rb_ref, h):
    out = jnp.zeros(bucket.shape, F32)
    for b in range(NUM_BUCKETS):
        out = jnp.where(bucket == b, rb_ref[b, h], out)
    return out


def _bias_tiles_body(rb_ref, o_ref, bk_ref):
    cls = pl.program_id(0)
    h = pl.program_id(1)

    @pl.when(h == 0)
    def _():
        r = lax.broadcasted_iota(jnp.int32, bk_ref.shape, 0)
        c = lax.broadcasted_iota(jnp.int32, bk_ref.shape, 1)
        d = jnp.maximum(cls * ATTN_TILE + r - c, 0)
        bk_ref[...] = _t5_bucket(d)

    o_ref[...] = _bias_lookup(bk_ref[...], rb_ref, h)


def _bias_tiles(rel_bias):
    t = ATTN_TILE
    return pl.pallas_call(
        _bias_tiles_body,
        grid=(3, DIFF_HEADS),
        in_specs=[pl.BlockSpec(memory_space=pltpu.SMEM)],
        out_specs=pl.BlockSpec((None, None, t, t), lambda c, h: (h, c, 0, 0)),
        out_shape=jax.ShapeDtypeStruct((DIFF_HEADS, 3, t, t), F32),
        scratch_shapes=[pltpu.VMEM((t, t), jnp.int32)],
        compiler_params=_params(("arbitrary", "arbitrary")),
        name="bias_tiles",
    )(rel_bias)


def _bias_decode_body(rb_ref, o_ref):
    c = lax.broadcasted_iota(jnp.int32, (1, 3 * LANES), 1)
    d = jnp.where(c < LANES, PAGE_SIZE - c, jnp.where(c < 2 * LANES, PAST_LEN, 0))
    bucket = _t5_bucket(d)
    for r in range(2 * DIFF_HEADS):
        head = (r // (2 * DIFF_GROUP)) * DIFF_GROUP + r % DIFF_GROUP
        o_ref[r:r + 1, :] = _bias_lookup(bucket, rb_ref, head)


def _bias_decode(rel_bias):
    return pl.pallas_call(
        _bias_decode_body,
        in_specs=[pl.BlockSpec(memory_space=pltpu.SMEM)],
        out_shape=jax.ShapeDtypeStruct((2 * DIFF_HEADS, 3 * LANES), F32),
        name="bias_decode",
    )(rel_bias)


def _diff_proj_body(x_ref, g_ref, w_ref, o_ref, h_ref):
    @pl.when(pl.program_id(1) == 0)
    def _():
        h_ref[...] = _rms(x_ref[...], g_ref[...]).astype(BF16)

    o_ref[...] = _dot(h_ref[...], w_ref[...])


def _diff_proj(x, g, w_in):
    n = x.shape[0]
    tm = min(512, n)
    tn = 512
    nout = w_in.shape[1]
    return pl.pallas_call(
        _diff_proj_body,
        grid=(n // tm, nout // tn),
        in_specs=[
            pl.BlockSpec((tm, D_MODEL), lambda i, j: (i, 0)),
            pl.BlockSpec((1, D_MODEL), lambda i, j: (0, 0)),
            pl.BlockSpec((D_MODEL, tn), lambda i, j: (0, j)),
        ],
        out_specs=pl.BlockSpec((tm, tn), lambda i, j: (i, j)),
        out_shape=jax.ShapeDtypeStruct((n, nout), F32),
        scratch_shapes=[pltpu.VMEM((tm, D_MODEL), BF16)],
        compiler_params=_params(("parallel", "arbitrary")),
        name="diff_proj",
    )(x, g, w_in)


def _lam_full(lam_ref, lam_init):
    lf = lam_ref[...]
    a = jnp.sum(lf[0:1] * lf[1:2], axis=-1, keepdims=True)
    b = jnp.sum(lf[2:3] * lf[3:4], axis=-1, keepdims=True)
    return jnp.exp(a) - jnp.exp(b) + lam_init


def _diff_combine(o0, o1, lam, subln, lam_init):
    a = o0 - lam * o1
    return _rms(a, subln) * (1.0 - lam_init)


def _diff_attn_body(qi_ref, ki_ref, q_ref, k_ref, v_ref, bias_ref, lam_ref, sub_ref, o_ref,
                    q0_ref, q1_ref, m_ref, l_ref, acc_ref, *, lam_init):
    t = pl.program_id(2)
    qi = qi_ref[t]
    ki = ki_ref[t]

    @pl.when(ki == 0)
    def _():
        q = (q_ref[...] * DIFF_SCALE).astype(BF16)
        lane = lax.broadcasted_iota(jnp.int32, q.shape, 1)
        zero = jnp.zeros(q.shape, BF16)
        q0_ref[...] = jnp.where(lane < DIFF_HEAD, q, zero)
        q1_ref[...] = jnp.where(lane < DIFF_HEAD, zero, q)
        m_ref[...] = jnp.full(m_ref.shape, NEG_INF, F32)
        l_ref[...] = jnp.zeros(l_ref.shape, F32)
        acc_ref[...] = jnp.zeros(acc_ref.shape, F32)

    k = k_ref[...].astype(BF16)
    v_aug = _with_ones(v_ref[...].astype(BF16))
    bias = bias_ref[...]

    def update(masked):
        for mp, qm_ref in enumerate((q0_ref, q1_ref)):
            s = _dot_nt(qm_ref[...], k) + bias
            if masked:
                s = _causal_mask(s)
            _online(s, v_aug, m_ref.at[mp], l_ref.at[mp], acc_ref.at[mp])

    @pl.when(ki < qi)
    def _():
        update(False)

    @pl.when(ki == qi)
    def _():
        update(True)
        lam = _lam_full(lam_ref, lam_init)
        o0 = acc_ref[0] / l_ref[0]
        o1 = acc_ref[1] / l_ref[1]
        o_ref[...] = _diff_combine(o0, o1, lam, sub_ref[...], lam_init).astype(BF16)


def _diff_attn(qkv, bias_tiles, lam, subln, lam_init):
    t = ATTN_TILE
    nblk = SEQ // t
    qi, ki = _tri_schedule(nblk)
    kcol0 = DIFF_Q // LANES
    vcol0 = (DIFF_Q + DIFF_K) // LANES
    grid_spec = pltpu.PrefetchScalarGridSpec(
        num_scalar_prefetch=2,
        grid=(BATCH, DIFF_HEADS, qi.shape[0]),
        in_specs=[
            pl.BlockSpec((t, LANES), lambda b, h, s, qi, ki: (b * nblk + qi[s], h)),
            pl.BlockSpec((t, LANES),
                         lambda b, h, s, qi, ki: (b * nblk + ki[s], kcol0 + h // DIFF_GROUP)),
            pl.BlockSpec((t, LANES),
                         lambda b, h, s, qi, ki: (b * nblk + ki[s], vcol0 + h // DIFF_GROUP)),
            pl.BlockSpec((None, None, t, t),
                         lambda b, h, s, qi, ki: (h, jnp.minimum(qi[s] - ki[s], 2), 0, 0)),
            pl.BlockSpec((4, DIFF_HEAD), lambda b, h, s, qi, ki: (0, 0)),
            pl.BlockSpec((1, DIFF_VHEAD), lambda b, h, s, qi, ki: (0, 0)),
        ],
        out_specs=pl.BlockSpec((t, LANES), lambda b, h, s, qi, ki: (b * nblk + qi[s], h)),
        scratch_shapes=[pltpu.VMEM((t, LANES), BF16), pltpu.VMEM((t, LANES), BF16),
                        pltpu.VMEM((2, t, LANES), F32), pltpu.VMEM((2, t, LANES), F32),
                        pltpu.VMEM((2, t, DIFF_VHEAD), F32)],
    )
    return pl.pallas_call(
        functools.partial(_diff_attn_body, lam_init=lam_init),
        grid_spec=grid_spec,
        out_shape=jax.ShapeDtypeStruct((BATCH * SEQ, DIFF_HEADS * DIFF_VHEAD), BF16),
        compiler_params=_params(("parallel", "parallel", "arbitrary")),
        name="diff_attn",
    )(qi, ki, qkv, qkv, qkv, bias_tiles, lam, subln)


def _diff_side(page_table, row0, n_rows, q32, k_new, v_new, dec_bias, lam, subln, cache_kt,
               cache_v2, lam_init):
    nrow = 2 * DIFF_HEADS
    half = nrow // 2
    rows = slice(row0, row0 + n_rows)
    inputs = (q32[rows], k_new[rows].reshape(n_rows, 1, DIFF_K), v_new[rows].reshape(n_rows, 1, DIFF_V),
              dec_bias, lam, subln, cache_kt, cache_v2)
    in_specs = tuple(_resident(a.shape) for a in inputs[:6]) + (
        pl.BlockSpec(memory_space=pl.ANY), pl.BlockSpec(memory_space=pl.ANY))
    scratch = (pltpu.VMEM((2, PAGES_PER_STEP, DIFF_K, PAGE_SIZE), F32),
               pltpu.VMEM((2, PAGES_PER_STEP, 2 * PAGE_SIZE, DIFF_VHEAD), F32),
               pltpu.VMEM((DIFF_K, KEYS_PER_STEP), BF16),
               pltpu.VMEM((KEYS_PER_STEP, DIFF_VHEAD), BF16),
               pltpu.VMEM((KEYS_PER_STEP, DIFF_VHEAD), BF16),
               pltpu.VMEM((nrow, DIFF_K), BF16),
               pltpu.VMEM((nrow, LANES), F32), pltpu.VMEM((nrow, LANES), F32),
               pltpu.VMEM((nrow, DIFF_VHEAD), F32),
               pltpu.SemaphoreType.DMA((2,)))

    def begin(step, pt_ref, in_refs, scratch_refs):
        _side_begin(step, pt_ref, in_refs[6:8], scratch_refs[:2], scratch_refs[-1], row0)

    def run(step, pt_ref, in_refs, o_ref, scratch_refs):
        q_ref, knew_ref, vnew_ref, bias_ref, lam_ref, sub_ref, kt_hbm, v_hbm = in_refs
        kbuf, vbuf, ktb_ref, v0_ref, v1_ref, qbig_ref, m_ref, l_ref, acc_ref, sem = scratch_refs

        def unit(row, sub, i, slot):
            def init():
                r = lax.broadcasted_iota(jnp.int32, (nrow, DIFF_K), 0)
                c = lax.broadcasted_iota(jnp.int32, (nrow, DIFF_K), 1)
                q = q_ref[row] * DIFF_SCALE
                rep = jnp.concatenate([q, q, q, q], axis=-1)
                qbig_ref[...] = jnp.where((c // DIFF_HEAD) == (r // DIFF_GROUP), rep, 0.0).astype(BF16)
                m_ref[...] = jnp.full(m_ref.shape, NEG_INF, F32)
                l_ref[...] = jnp.zeros(l_ref.shape, F32)
                acc_ref[...] = jnp.zeros(acc_ref.shape, F32)

            _when_first_group(sub, i, init)
            for k in range(PAGES_PER_STEP):
                keys = slice(k * PAGE_SIZE, (k + 1) * PAGE_SIZE)
                ktb_ref[:, keys] = kbuf[slot, k].astype(BF16)
                v0_ref[keys, :] = vbuf[slot, k, pl.ds(0, PAGE_SIZE, stride=2), :].astype(BF16)
                v1_ref[keys, :] = vbuf[slot, k, pl.ds(1, PAGE_SIZE, stride=2), :].astype(BF16)
            qbig = qbig_ref[...]
            far = bias_ref[:, LANES:2 * LANES]
            s = _dot(qbig, ktb_ref[...]) + _lane_tile(far, KEYS_PER_STEP)
            if i == UNITS_PER_STEP - 1:
                near_delta = bias_ref[:, :LANES] - far
                tail = (s[:, KEYS_PER_STEP - PAGE_SIZE:]
                        + jnp.where(sub == STEPS_PER_ROW - 1, near_delta, 0.0))
                s = jnp.concatenate([s[:, :KEYS_PER_STEP - PAGE_SIZE], tail], axis=1)
            m, corr, p, l = _softmax_step(s, m_ref[...], l_ref[...])
            pb = p.astype(BF16)
            pv = jnp.concatenate([_dot(pb[:half], v0_ref[...]), _dot(pb[half:], v1_ref[...])], axis=0)
            acc = acc_ref[...] * corr + pv
            m_ref[...] = m
            l_ref[...] = l
            acc_ref[...] = acc

            def finish():
                kn = knew_ref[row].astype(BF16).astype(F32)
                vn = vnew_ref[row].astype(BF16).astype(F32)
                s_new = (jnp.sum(qbig.astype(F32) * kn, axis=-1, keepdims=True)
                         + bias_ref[:, 2 * LANES:])
                m_new = jnp.maximum(m, s_new)
                c2 = jnp.exp(m - m_new)
                p_new = jnp.exp(s_new - m_new)
                l2 = l * c2 + p_new
                rv = lax.broadcasted_iota(jnp.int32, (nrow, DIFF_VHEAD), 0)
                v_row = jnp.where(rv < half, vn[:, :DIFF_VHEAD], vn[:, DIFF_VHEAD:])
                o = (acc * c2 + p_new.astype(BF16).astype(F32) * v_row) / l2
                g8 = DIFF_GROUP
                o0 = jnp.concatenate([o[0:g8], o[2 * g8:3 * g8]], axis=0)
                o1 = jnp.concatenate([o[g8:2 * g8], o[3 * g8:4 * g8]], axis=0)
                lam_full = _lam_full(lam_ref, lam_init)
                o_ref[row] = _diff_combine(o0, o1, lam_full, sub_ref[...], lam_init).astype(BF16)

            _when_last_group(sub, i, finish)

        _side_units(step, pt_ref, (kt_hbm, v_hbm), (kbuf, vbuf), sem, row0, n_rows, unit)

    return _Side(page_table.reshape(-1), inputs, in_specs,
                 jax.ShapeDtypeStruct((n_rows, DIFF_HEADS, DIFF_VHEAD), BF16), scratch, begin, run)


def _rot_half_cols(w):
    half = w.shape[-1] // 2
    return jnp.concatenate([-w[..., half:], w[..., :half]], axis=-1)


def _pad_cols(w, width):
    return jnp.pad(w, [(0, 0)] * (w.ndim - 1) + [(0, width - w.shape[-1])])


def _prep_mla(w_in, q_norm, kv_norm, w_q_up, w_kv_up, w_out):
    w_kr = w_in[:, Q_LORA + KV_LORA:]
    w_in2 = jnp.concatenate([w_in[:, :Q_LORA + KV_LORA], _pad_cols(w_kr, LANES),
                             _pad_cols(_rot_half_cols(w_kr), LANES)], axis=1)
    wq = w_q_up.reshape(Q_LORA, MLA_HEADS, QK_NOPE + QK_ROPE)
    wr = wq[:, :, QK_NOPE:]
    hw = MLA_HEADS * LANES
    wkv = w_kv_up.reshape(KV_LORA, MLA_HEADS, QK_NOPE + V_HEAD)
    return {
        "w_in": w_in2.astype(BF16),
        "q_norm": q_norm.reshape(1, Q_LORA),
        "kv_norm": kv_norm.reshape(1, KV_LORA),
        "w_qn": wq[:, :, :QK_NOPE].reshape(Q_LORA, hw).astype(BF16),
        "w_qa": _pad_cols(wr, LANES).reshape(Q_LORA, hw).astype(BF16),
        "w_qb": _pad_cols(_rot_half_cols(wr), LANES).reshape(Q_LORA, hw).astype(BF16),
        "w_kv": w_kv_up.astype(BF16),
        "w_ukt": jnp.transpose(wkv[:, :, :QK_NOPE], (1, 2, 0)).astype(BF16),
        "w_uv": jnp.transpose(wkv[:, :, QK_NOPE:], (1, 0, 2)).astype(BF16),
        "w_out": w_out.astype(BF16),
    }


def _rope_tables(pos):
    inv = ROPE_THETA ** (-jnp.arange(0, QK_ROPE, 2, dtype=F32) / QK_ROPE)
    ang = pos.astype(F32)[:, None] * inv[None, :]
    reps = LANES // (QK_ROPE // 2)
    return jnp.tile(jnp.cos(ang), (1, reps)), jnp.tile(jnp.sin(ang), (1, reps))


def kernel(x_prompt, x_sample, cache_mla_ckv, cache_mla_krope, cache_diff_k, cache_diff_v, page_table, ffn_norm, w_ffn_up, w_ffn_down, mix_norm, mla_w_in, mla_q_norm, mla_kv_norm, mla_w_q_up, mla_w_kv_up, mla_w_out, diff_w_in, diff_lambda, diff_subln, diff_w_out, rel_bias, final_norm):
    xp = x_prompt.reshape(BATCH * SEQ, D_MODEL)
    xs = x_sample.reshape(DEC_BATCH, D_MODEL)
    w_up = w_ffn_up.astype(BF16)
    w_down = w_ffn_down.astype(BF16)
    gf = final_norm.reshape(1, D_MODEL)
    cos_p, sin_p = _rope_tables(jnp.arange(SEQ, dtype=jnp.int32))
    cos_s, sin_s = _rope_tables(jnp.full((1,), PAST_LEN, jnp.int32))
    n_phys = cache_mla_ckv.shape[1]

    def ffn(x, i, k, final=False, side=None):
        return _ffn(x, ffn_norm[i, k].reshape(1, D_MODEL), w_up, w_down, gf, i, k, final=final,
                    side=side)

    cache_krt = jnp.transpose(cache_mla_krope, (0, 1, 3, 2))
    cache_kt = jnp.transpose(cache_diff_k, (0, 1, 3, 4, 5, 2)).reshape(1, n_phys, DIFF_K, PAGE_SIZE)
    cache_v2 = cache_diff_v.reshape(1, n_phys, 2 * PAGE_SIZE, DIFF_VHEAD)
    half = DEC_BATCH // 2

    xs = ffn(xs, 0, 0)
    g0 = mix_norm[0].reshape(1, D_MODEL)
    wm = _prep_mla(mla_w_in[0], mla_q_norm[0], mla_kv_norm[0], mla_w_q_up[0], mla_w_kv_up[0],
                   mla_w_out[0])
    ckv_s, kr_s, qn_s, qr_s, _, _ = _mla_proj(xs, g0, wm, cos_s, sin_s)
    q_dec = jnp.transpose(_mla_qlat(qn_s, qr_s, wm["w_ukt"]), (1, 0, 2))

    def mla_side(row0):
        return _mla_side(page_table, row0, half, q_dec, ckv_s, kr_s, cache_mla_ckv, cache_krt)

    xp, o_lat_a = ffn(xp, 0, 0, side=mla_side(0))
    ckv_p, kr_p, qn_p, qr_p, kv_p, krb_p = _mla_proj(xp, g0, wm, cos_p, sin_p)
    o_p = _mla_attn(qn_p, qr_p, kv_p, krb_p)
    xp = _outproj(xp, o_p, wm["w_out"])
    xp, o_lat_b = ffn(xp, 0, 1, side=mla_side(half))

    o_lat = jnp.concatenate([o_lat_a, o_lat_b], axis=0)
    xs = _mla_finish(xs, jnp.transpose(o_lat, (1, 0, 2)), wm["w_uv"], wm["w_out"])
    xs = ffn(xs, 0, 1)

    xs = ffn(xs, 1, 0)
    lam_init = 0.8 - 0.6 * math.exp(-0.3 * 1)
    g1 = mix_norm[1].reshape(1, D_MODEL)
    w_din = diff_w_in[0].astype(BF16)
    w_dout = diff_w_out[0].astype(BF16)
    lam = diff_lambda[0]
    subln = diff_subln[0].reshape(1, DIFF_VHEAD)
    bias_tiles = _bias_tiles(rel_bias)
    dec_bias = _bias_decode(rel_bias)

    qkv_s = _diff_proj(xs, g1, w_din)
    q32 = qkv_s[:, :DIFF_Q].reshape(DEC_BATCH, DIFF_KV_HEADS, DIFF_GROUP, 2, DIFF_HEAD)
    q32 = jnp.transpose(q32, (0, 1, 3, 2, 4)).reshape(DEC_BATCH, 2 * DIFF_HEADS, DIFF_HEAD)
    k_s = qkv_s[:, DIFF_Q:DIFF_Q + DIFF_K]
    v_s = qkv_s[:, DIFF_Q + DIFF_K:]

    def diff_side(row0):
        return _diff_side(page_table, row0, half, q32, k_s, v_s, dec_bias, lam, subln, cache_kt,
                          cache_v2, lam_init)

    xp, a_s_a = ffn(xp, 1, 0, side=diff_side(0))
    qkv_p = _diff_proj(xp, g1, w_din)
    a_p = _diff_attn(qkv_p, bias_tiles, lam, subln, lam_init)
    xp = _outproj(xp, a_p, w_dout)
    xp, a_s_b = ffn(xp, 1, 1, final=True, side=diff_side(half))

    a_s = jnp.concatenate([a_s_a, a_s_b], axis=0)
    xs = _outproj(xs, a_s.reshape(DEC_BATCH, DIFF_HEADS * DIFF_VHEAD), w_dout)
    xs = ffn(xs, 1, 1, final=True)

    k_p = qkv_p[:, DIFF_Q:DIFF_Q + DIFF_K]
    v_p = qkv_p[:, DIFF_Q + DIFF_K:]
    return (
        xp.reshape(BATCH, SEQ, D_MODEL),
        xs.reshape(DEC_BATCH, 1, D_MODEL),
        ckv_p.reshape(1, BATCH, SEQ, KV_LORA),
        kr_p.reshape(1, BATCH, SEQ, QK_ROPE),
        k_p.reshape(1, BATCH, SEQ, DIFF_KV_HEADS, 2, DIFF_HEAD),
        v_p.reshape(1, BATCH, SEQ, DIFF_KV_HEADS, DIFF_VHEAD),
        ckv_s.reshape(1, DEC_BATCH, 1, KV_LORA),
        kr_s.reshape(1, DEC_BATCH, 1, QK_ROPE),
        k_s.reshape(1, DEC_BATCH, 1, DIFF_KV_HEADS, 2, DIFF_HEAD),
        v_s.reshape(1, DEC_BATCH, 1, DIFF_KV_HEADS, DIFF_VHEAD),
    )
```

```python
import functools
import math
from typing import Any, Callable, NamedTuple

import jax
import jax.numpy as jnp
import numpy as np
from jax import lax
from jax.experimental import pallas as pl
from jax.experimental.pallas import tpu as pltpu

D_MODEL = 2048
BATCH = 4
SEQ = 2048
DEPTH = 2
DEC_BATCH = 128
PAST_LEN = 16384
PAGE_SIZE = 128
N_PAGES = PAST_LEN // PAGE_SIZE
D_FF = 5632
EPS = 1e-6
MLA_HEADS = 16
Q_LORA = 512
KV_LORA = 512
QK_NOPE = 128
QK_ROPE = 64
V_HEAD = 128
MLA_SCALE = (QK_NOPE + QK_ROPE) ** -0.5
ROPE_THETA = 10000.0
DIFF_HEADS = 16
DIFF_KV_HEADS = 2
DIFF_GROUP = DIFF_HEADS // DIFF_KV_HEADS
DIFF_HEAD = 64
DIFF_VHEAD = 128
DIFF_Q = DIFF_HEADS * 2 * DIFF_HEAD
DIFF_K = DIFF_KV_HEADS * 2 * DIFF_HEAD
DIFF_V = DIFF_KV_HEADS * DIFF_VHEAD
DIFF_SCALE = DIFF_HEAD ** -0.5
NUM_BUCKETS = 32
MAX_DISTANCE = 128
NEG_INF = -1e30

LANES = 128
V7X_VMEM_LIMIT_BYTES = 60000 * 1024

FFN_ROW_TILE = 512
FFN_FF_TILE = 512
PROJ_ROW_TILE = 256
ATTN_TILE = 512
ATTN_ROW_CHUNK = 256
PAGES_PER_STEP = 8
KEYS_PER_STEP = PAGES_PER_STEP * PAGE_SIZE
PAGE_GROUPS = N_PAGES // PAGES_PER_STEP
UNITS_PER_STEP = 8
STEPS_PER_ROW = PAGE_GROUPS // UNITS_PER_STEP
RING_SLOTS = 4
PREFETCH = RING_SLOTS - 1
assert UNITS_PER_STEP % RING_SLOTS == 0
assert math.frexp(DIFF_SCALE)[0] == 0.5
BF16 = jnp.bfloat16
F32 = jnp.float32


def _params(sem):
    return pltpu.CompilerParams(dimension_semantics=sem, vmem_limit_bytes=V7X_VMEM_LIMIT_BYTES)


def _rms(x, g):
    return x * lax.rsqrt(jnp.mean(x * x, axis=-1, keepdims=True) + EPS) * g


def _dot(a, b):
    return jnp.dot(a, b, preferred_element_type=F32)


def _dot_nt(a, b):
    return lax.dot_general(a, b, (((1,), (1,)), ((), ())), preferred_element_type=F32)


def _resident(shape):
    nd = len(shape)
    return pl.BlockSpec(shape, lambda *_: (0,) * nd, pipeline_mode=pl.Buffered(1))


class _Side(NamedTuple):
    page_table: Any
    inputs: tuple
    in_specs: tuple
    out_shape: Any
    scratch: tuple
    begin: Callable
    run: Callable


def _ffn_body(*refs, final, side):
    n_side_in = len(side.inputs) if side else 0
    pt_ref = refs[0] if side else None
    refs = refs[1:] if side else refs
    x_ref, g_ref, wg_ref, wu_ref, wd_ref, gf_ref = refs[:6]
    side_in = refs[6:6 + n_side_in]
    o_ref = refs[6 + n_side_in]
    rest = refs[7 + n_side_in:]
    side_out, rest = (rest[0], rest[1:]) if side else (None, rest)
    h_ref, side_scratch = rest[0], rest[1:]
    j = pl.program_id(1)
    step = pl.program_id(0) * pl.num_programs(1) + j

    if side:
        side.begin(step, pt_ref, side_in, side_scratch)

    @pl.when(j == 0)
    def _():
        x = x_ref[...]
        h_ref[...] = _rms(x, g_ref[...]).astype(BF16)
        o_ref[...] = x

    h = h_ref[...]
    gate = _dot(h, wg_ref[...])
    up = _dot(h, wu_ref[...])
    a = (gate / (1.0 + jnp.exp(-gate)) * up).astype(BF16)
    o_ref[...] += 0.5 * _dot(a, wd_ref[...])

    if final:
        @pl.when(j == pl.num_programs(1) - 1)
        def _():
            o_ref[...] = _rms(o_ref[...], gf_ref[...])

    if side:
        side.run(step, pt_ref, side_in, side_out, side_scratch)


def _ffn(x, g, w_up, w_down, gf, layer, pos, *, final, side=None):
    n = x.shape[0]
    tm = min(FFN_ROW_TILE, n)
    tf = FFN_FF_TILE
    nf = D_FF // tf
    in_specs = [
        pl.BlockSpec((tm, D_MODEL), lambda i, j, *_: (i, 0)),
        pl.BlockSpec((1, D_MODEL), lambda i, j, *_: (0, 0)),
        pl.BlockSpec((None, None, D_MODEL, tf), lambda i, j, *_: (layer, pos, 0, j)),
        pl.BlockSpec((None, None, D_MODEL, tf), lambda i, j, *_: (layer, pos, 0, j + nf)),
        pl.BlockSpec((None, None, tf, D_MODEL), lambda i, j, *_: (layer, pos, j, 0)),
        pl.BlockSpec((1, D_MODEL), lambda i, j, *_: (0, 0)),
    ]
    out_specs = pl.BlockSpec((tm, D_MODEL), lambda i, j, *_: (i, 0))
    out_shape = jax.ShapeDtypeStruct((n, D_MODEL), F32)
    scratch = [pltpu.VMEM((tm, D_MODEL), BF16)]
    operands = (x, g, w_up, w_up, w_down, gf)
    if side:
        assert (n // tm) * nf >= side.out_shape.shape[0] * PAGE_GROUPS // UNITS_PER_STEP
        in_specs += list(side.in_specs)
        nd_side = len(side.out_shape.shape)
        out_specs = [out_specs, pl.BlockSpec(side.out_shape.shape, lambda *_: (0,) * nd_side)]
        out_shape = [out_shape, side.out_shape]
        scratch += list(side.scratch)
        operands = (side.page_table,) + operands + tuple(side.inputs)
    grid_spec = pltpu.PrefetchScalarGridSpec(
        num_scalar_prefetch=1 if side else 0, grid=(n // tm, nf),
        in_specs=in_specs, out_specs=out_specs, scratch_shapes=scratch)
    sem = ("arbitrary", "arbitrary") if side else ("parallel", "arbitrary")
    return pl.pallas_call(
        functools.partial(_ffn_body, final=final, side=side),
        grid_spec=grid_spec,
        out_shape=out_shape,
        compiler_params=_params(sem),
        name="ffn_decode" if side else "ffn",
    )(*operands)


def _outproj_body(x_ref, a_ref, w_ref, o_ref):
    o_ref[...] = x_ref[...] + _dot(a_ref[...], w_ref[...])


def _outproj(x, a, w):
    n = x.shape[0]
    tm = min(512, n)
    return pl.pallas_call(
        _outproj_body,
        grid=(n // tm,),
        in_specs=[
            pl.BlockSpec((tm, D_MODEL), lambda i: (i, 0)),
            pl.BlockSpec((tm, a.shape[1]), lambda i: (i, 0)),
            _resident(w.shape),
        ],
        out_specs=pl.BlockSpec((tm, D_MODEL), lambda i: (i, 0)),
        out_shape=jax.ShapeDtypeStruct((n, D_MODEL), F32),
        compiler_params=_params(("parallel",)),
        name="outproj",
    )(x, a, w)


def _mla_proj_body(x_ref, g_ref, win_ref, qn_ref, kvn_ref, cos_ref, sin_ref,
                   wqn_ref, wqa_ref, wqb_ref, wkv_ref,
                   ckv_ref, kr_ref, qnope_ref, qrope_ref, kv_ref, krb_ref):
    h = _rms(x_ref[...], g_ref[...]).astype(BF16)
    proj = _dot(h, win_ref[...])
    q_c = _rms(proj[:, :Q_LORA], qn_ref[...]).astype(BF16)
    ckv = _rms(proj[:, Q_LORA:Q_LORA + KV_LORA], kvn_ref[...])
    cos = cos_ref[...]
    sin = sin_ref[...]
    kr = proj[:, 1024:1152] * cos + proj[:, 1152:1280] * sin
    ckv_ref[...] = ckv
    kr_ref[...] = kr[:, :QK_ROPE]
    ckv_b = ckv.astype(BF16)
    krb_ref[...] = kr.astype(BF16)
    qnope_ref[...] = _dot(q_c, wqn_ref[...]).astype(BF16)
    qa = _dot(q_c, wqa_ref[...])
    qb = _dot(q_c, wqb_ref[...])
    for hd in range(MLA_HEADS):
        sl = slice(hd * LANES, (hd + 1) * LANES)
        qrope_ref[:, sl] = (qa[:, sl] * cos + qb[:, sl] * sin).astype(BF16)
    kv_ref[...] = _dot(ckv_b, wkv_ref[...]).astype(BF16)


def _mla_proj(x, g, w, cos_t, sin_t):
    n = x.shape[0]
    tm = min(PROJ_ROW_TILE, n)
    tt = cos_t.shape[0]
    tb = tm if tt > 1 else 1
    nt = tt // tb
    row = lambda i: (i, 0)
    hw = MLA_HEADS * LANES
    outs = [
        jax.ShapeDtypeStruct((n, KV_LORA), F32),
        jax.ShapeDtypeStruct((n, QK_ROPE), F32),
        jax.ShapeDtypeStruct((n, hw), BF16),
        jax.ShapeDtypeStruct((n, hw), BF16),
        jax.ShapeDtypeStruct((n, 2 * hw), BF16),
        jax.ShapeDtypeStruct((n, LANES), BF16),
    ]
    return pl.pallas_call(
        _mla_proj_body,
        grid=(n // tm,),
        in_specs=[
            pl.BlockSpec((tm, D_MODEL), row),
            _resident((1, D_MODEL)),
            _resident(w["w_in"].shape),
            _resident((1, Q_LORA)),
            _resident((1, KV_LORA)),
            pl.BlockSpec((tb, LANES), lambda i: (i % nt, 0)),
            pl.BlockSpec((tb, LANES), lambda i: (i % nt, 0)),
            _resident(w["w_qn"].shape),
            _resident(w["w_qa"].shape),
            _resident(w["w_qb"].shape),
            _resident(w["w_kv"].shape),
        ],
        out_specs=[pl.BlockSpec((tm, s.shape[1]), row) for s in outs],
        out_shape=outs,
        compiler_params=_params(("parallel",)),
        name="mla_proj",
    )(x, g, w["w_in"], w["q_norm"], w["kv_norm"], cos_t, sin_t,
      w["w_qn"], w["w_qa"], w["w_qb"], w["w_kv"])


def _tri_schedule(nblk):
    qi, ki = [], []
    for q in range(nblk):
        for k in range(q + 1):
            qi.append(q)
            ki.append(k)
    return jnp.asarray(qi, jnp.int32), jnp.asarray(ki, jnp.int32)


def _lane_tile(x, width):
    return jnp.concatenate([x] * (width // LANES), axis=1)


def _with_ones(v):
    return jnp.concatenate([v, jnp.ones(v.shape, BF16)], axis=1)


def _online(s, v_aug, m_ref, l_ref, acc_ref):
    m_old = m_ref[...]
    m_new = jnp.maximum(m_old, jnp.max(s, axis=-1, keepdims=True))
    corr = jnp.exp(m_old - m_new)
    p = jnp.exp(s - _lane_tile(m_new, s.shape[1]))
    pv = _dot(p.astype(BF16), v_aug)
    l_ref[...] = l_ref[...] * corr + pv[:, LANES:]
    acc_ref[...] = acc_ref[...] * corr + pv[:, :LANES]
    m_ref[...] = m_new


def _attend_tile(score_fn, v_aug, m_ref, l_ref, acc_ref, masked, chunk):
    t = v_aug.shape[0]
    for r0 in range(0, t, chunk):
        nk = r0 + chunk if masked else t
        rows = slice(r0, r0 + chunk)
        s = score_fn(rows, nk)
        if masked:
            r = lax.broadcasted_iota(jnp.int32, s.shape, 0) + r0
            c = lax.broadcasted_iota(jnp.int32, s.shape, 1)
            s = jnp.where(c <= r, s, NEG_INF)
        _online(s, v_aug[:nk], m_ref.at[rows], l_ref.at[rows], acc_ref.at[rows])


def _mla_attn_body(qi_ref, ki_ref, qn_ref, qr_ref, kn_ref, kr_ref, v_ref, o_ref,
                   q_ref, m_ref, l_ref, acc_ref):
    t = pl.program_id(2)
    qi = qi_ref[t]
    ki = ki_ref[t]

    @pl.when(ki == 0)
    def _():
        q_ref[...] = jnp.concatenate([qn_ref[...], qr_ref[...]], axis=1)
        m_ref[...] = jnp.full(m_ref.shape, NEG_INF, F32)
        l_ref[...] = jnp.zeros(l_ref.shape, F32)
        acc_ref[...] = jnp.zeros(acc_ref.shape, F32)

    k = jnp.concatenate([kn_ref[...], kr_ref[...]], axis=1)
    s = _dot_nt(q_ref[...], k) * MLA_SCALE
    v_aug = _with_ones(v_ref[...])

    def update(masked):
        _attend_tile(lambda rows, nk: s[rows, :nk], v_aug, m_ref, l_ref, acc_ref, masked, ATTN_TILE)

    @pl.when(ki < qi)
    def _():
        update(False)

    @pl.when(ki == qi)
    def _():
        update(True)
        o_ref[...] = (acc_ref[...] / l_ref[...]).astype(BF16)


def _mla_attn(q_nope, q_rope, kv, kr_b):
    t = ATTN_TILE
    nblk = SEQ // t
    qi, ki = _tri_schedule(nblk)
    qmap = lambda b, h, s, qi, ki: (b * nblk + qi[s], h)
    grid_spec = pltpu.PrefetchScalarGridSpec(
        num_scalar_prefetch=2,
        grid=(BATCH, MLA_HEADS, qi.shape[0]),
        in_specs=[
            pl.BlockSpec((t, LANES), qmap),
            pl.BlockSpec((t, LANES), qmap),
            pl.BlockSpec((t, LANES), lambda b, h, s, qi, ki: (b * nblk + ki[s], 2 * h)),
            pl.BlockSpec((t, LANES), lambda b, h, s, qi, ki: (b * nblk + ki[s], 0)),
            pl.BlockSpec((t, LANES), lambda b, h, s, qi, ki: (b * nblk + ki[s], 2 * h + 1)),
        ],
        out_specs=pl.BlockSpec((t, LANES), qmap),
        scratch_shapes=[pltpu.VMEM((t, 2 * LANES), BF16),
                        pltpu.VMEM((t, LANES), F32), pltpu.VMEM((t, LANES), F32),
                        pltpu.VMEM((t, V_HEAD), F32)],
    )
    return pl.pallas_call(
        _mla_attn_body,
        grid_spec=grid_spec,
        out_shape=jax.ShapeDtypeStruct((BATCH * SEQ, MLA_HEADS * V_HEAD), BF16),
        compiler_params=_params(("parallel", "parallel", "arbitrary")),
        name="mla_attn",
    )(qi, ki, q_nope, q_rope, kv, kr_b, kv)


def _mla_qlat_body(qn_ref, qr_ref, wukt_ref, o_ref):
    for hd in range(MLA_HEADS):
        ql = _dot(qn_ref[:, hd * LANES:(hd + 1) * LANES], wukt_ref[hd])
        o_ref[hd, :, :KV_LORA] = ql.astype(BF16)
        o_ref[hd, :, KV_LORA:] = qr_ref[:, hd * LANES:(hd + 1) * LANES]


def _mla_qlat(q_nope, q_rope, w_ukt):
    return pl.pallas_call(
        _mla_qlat_body,
        out_shape=jax.ShapeDtypeStruct((MLA_HEADS, DEC_BATCH, KV_LORA + LANES), BF16),
        compiler_params=_params(None),
        name="mla_qlat",
    )(q_nope, q_rope, w_ukt)


def _group_copies(page_of, hbm_refs, bufs, sem, slot):
    copies = []
    for k in range(PAGES_PER_STEP):
        page = page_of(k)
        for hbm, buf in zip(hbm_refs, bufs):
            copies.append(pltpu.make_async_copy(hbm.at[0, page], buf.at[slot, k], sem.at[slot]))
    return copies


def _start_group(pt_ref, base, hbm_refs, bufs, sem, slot):
    copies = _group_copies(lambda k: pt_ref[base + k], hbm_refs, bufs, sem, slot)
    for i, c in enumerate(copies):
        c.start(priority=(i // len(hbm_refs)) % 2)


def _wait_group(hbm_refs, bufs, sem, slot):
    for c in _group_copies(lambda k: 0, hbm_refs, bufs, sem, slot):
        c.wait()


def _side_begin(step, pt_ref, hbm_refs, bufs, sem, row0):
    @pl.when(step == 0)
    def _():
        for d in range(PREFETCH):
            _start_group(pt_ref, row0 * N_PAGES + d * PAGES_PER_STEP, hbm_refs, bufs, sem, d)


def _side_units(step, pt_ref, hbm_refs, bufs, sem, row0, n_rows, unit_fn):
    n_steps = n_rows * STEPS_PER_ROW
    first_base = row0 * N_PAGES
    last_base = (row0 + n_rows) * N_PAGES - PAGES_PER_STEP

    @pl.when(step < n_steps)
    def _():
        for i in range(UNITS_PER_STEP):
            slot = i % RING_SLOTS
            nxt = first_base + (step * UNITS_PER_STEP + i + PREFETCH) * PAGES_PER_STEP
            _start_group(pt_ref, jnp.minimum(nxt, last_base), hbm_refs, bufs, sem,
                         (i + PREFETCH) % RING_SLOTS)
            _wait_group(hbm_refs, bufs, sem, slot)
            unit_fn(step // STEPS_PER_ROW, step % STEPS_PER_ROW, i, slot)

    @pl.when(step == n_steps - 1)
    def _():
        for d in range(PREFETCH):
            _wait_group(hbm_refs, bufs, sem, d)


def _when_first_group(sub, i, fn):
    if i == 0:
        pl.when(sub == 0)(fn)


def _when_last_group(sub, i, fn):
    if i == UNITS_PER_STEP - 1:
        pl.when(sub == STEPS_PER_ROW - 1)(fn)


def _softmax_step(s, m, l):
    m_new = jnp.maximum(m, jnp.max(s, axis=-1, keepdims=True))
    corr = jnp.exp(m - m_new)
    p = jnp.exp(s - _lane_tile(m_new, s.shape[1]))
    return m_new, corr, p, l * corr + jnp.sum(p, axis=-1, keepdims=True)


def _mla_side(page_table, row0, n_rows, q_dec, ckv_new, kr_new, cache_ckv, cache_krt):
    rows = slice(row0, row0 + n_rows)
    inputs = (q_dec[rows], ckv_new[rows].reshape(n_rows, 1, KV_LORA),
              kr_new[rows].reshape(n_rows, 1, QK_ROPE), cache_ckv, cache_krt)
    in_specs = (_resident(inputs[0].shape), _resident(inputs[1].shape), _resident(inputs[2].shape),
                pl.BlockSpec(memory_space=pl.ANY), pl.BlockSpec(memory_space=pl.ANY))
    scratch = (pltpu.VMEM((RING_SLOTS, PAGES_PER_STEP, PAGE_SIZE, KV_LORA), F32),
               pltpu.VMEM((RING_SLOTS, PAGES_PER_STEP, QK_ROPE, PAGE_SIZE), F32),
               pltpu.VMEM((KEYS_PER_STEP, KV_LORA), BF16),
               pltpu.VMEM((QK_ROPE, KEYS_PER_STEP), BF16),
               pltpu.VMEM((MLA_HEADS, LANES), F32), pltpu.VMEM((MLA_HEADS, LANES), F32),
               pltpu.VMEM((MLA_HEADS, KV_LORA), F32),
               pltpu.SemaphoreType.DMA((RING_SLOTS,)))

    def begin(step, pt_ref, in_refs, scratch_refs):
        cbuf, rbuf = scratch_refs[:2]
        _side_begin(step, pt_ref, in_refs[3:5], (cbuf, rbuf), scratch_refs[-1], row0)

    def run(step, pt_ref, in_refs, o_ref, scratch_refs):
        q_ref, cnew_ref, krnew_ref, ckv_hbm, krt_hbm = in_refs
        cbuf, rbuf, kc_ref, krt_ref, m_ref, l_ref, acc_ref, sem = scratch_refs

        def unit(row, sub, i, slot):
            def init():
                m_ref[...] = jnp.full(m_ref.shape, NEG_INF, F32)
                l_ref[...] = jnp.zeros(l_ref.shape, F32)
                acc_ref[...] = jnp.zeros(acc_ref.shape, F32)

            _when_first_group(sub, i, init)
            q = q_ref[row]
            q_lat = q[:, :KV_LORA]
            q_rope = q[:, KV_LORA:KV_LORA + QK_ROPE]
            for k in range(PAGES_PER_STEP):
                keys = slice(k * PAGE_SIZE, (k + 1) * PAGE_SIZE)
                kc_ref[keys, :] = cbuf[slot, k].astype(BF16)
                krt_ref[:, keys] = rbuf[slot, k].astype(BF16)
            kc = kc_ref[...]
            s = (_dot_nt(q_lat, kc) + _dot(q_rope, krt_ref[...])) * MLA_SCALE
            m, corr, p, l = _softmax_step(s, m_ref[...], l_ref[...])
            acc = acc_ref[...] * _lane_tile(corr, KV_LORA) + _dot(p.astype(BF16), kc)
            m_ref[...] = m
            l_ref[...] = l
            acc_ref[...] = acc

            def finish():
                cnew = cnew_ref[row].astype(BF16).astype(F32)
                krnew = krnew_ref[row].astype(BF16).astype(F32)
                s_new = (jnp.sum(q_lat.astype(F32) * cnew, axis=-1, keepdims=True)
                         + jnp.sum(q_rope.astype(F32) * krnew, axis=-1, keepdims=True)) * MLA_SCALE
                m_new = jnp.maximum(m, s_new)
                c2 = jnp.exp(m - m_new)
                p_new = jnp.exp(s_new - m_new)
                l2 = l * c2 + p_new
                acc2 = (acc * _lane_tile(c2, KV_LORA)
                        + _lane_tile(p_new.astype(BF16).astype(F32), KV_LORA) * cnew)
                o_ref[row] = acc2 / _lane_tile(l2, KV_LORA)

            _when_last_group(sub, i, finish)

        _side_units(step, pt_ref, (ckv_hbm, krt_hbm), (cbuf, rbuf), sem, row0, n_rows, unit)

    return _Side(page_table.reshape(-1), inputs, in_specs,
                 jax.ShapeDtypeStruct((n_rows, MLA_HEADS, KV_LORA), F32), scratch, begin, run)


def _mla_finish_body(x_ref, olat_ref, wuv_ref, wout_ref, o_ref, o_scr):
    for hd in range(MLA_HEADS):
        o_scr[:, hd * V_HEAD:(hd + 1) * V_HEAD] = _dot(
            olat_ref[hd].astype(BF16), wuv_ref[hd]).astype(BF16)
    o_ref[...] = x_ref[...] + _dot(o_scr[...], wout_ref[...])


def _mla_finish(x, o_lat, w_uv, w_out):
    return pl.pallas_call(
        _mla_finish_body,
        out_shape=jax.ShapeDtypeStruct((DEC_BATCH, D_MODEL), F32),
        scratch_shapes=[pltpu.VMEM((DEC_BATCH, MLA_HEADS * V_HEAD), BF16)],
        compiler_params=_params(None),
        name="mla_finish",
    )(x, o_lat, w_uv, w_out)


def _t5_bucket(d):
    max_exact = NUM_BUCKETS // 2
    nf = jnp.maximum(d, 1).astype(F32)
    large = max_exact + (jnp.log(nf / max_exact) / math.log(MAX_DISTANCE / max_exact)
                         * (NUM_BUCKETS - max_exact)).astype(jnp.int32)
    return jnp.where(d < max_exact, d, jnp.minimum(large, NUM_BUCKETS - 1))


def _bias_lookup(bucket, rb_ref, h):
    out = jnp.zeros(bucket.shape, F32)
    for b in range(NUM_BUCKETS):
        out = jnp.where(bucket == b, rb_ref[b, h], out)
    return out


def _bias_tiles_body(rb_ref, o_ref, bk_ref):
    cls = pl.program_id(0)
    h = pl.program_id(1)

    @pl.when(h == 0)
    def _():
        r = lax.broadcasted_iota(jnp.int32, bk_ref.shape, 0)
        c = lax.broadcasted_iota(jnp.int32, bk_ref.shape, 1)
        d = jnp.maximum(cls * ATTN_TILE + r - c, 0)
        bk_ref[...] = _t5_bucket(d)

    o_ref[...] = _bias_lookup(bk_ref[...], rb_ref, h)


def _bias_tiles(rel_bias):
    t = ATTN_TILE
    return pl.pallas_call(
        _bias_tiles_body,
        grid=(3, DIFF_HEADS),
        in_specs=[pl.BlockSpec(memory_space=pltpu.SMEM)],
        out_specs=pl.BlockSpec((None, None, t, t), lambda c, h: (h, c, 0, 0)),
        out_shape=jax.ShapeDtypeStruct((DIFF_HEADS, 3, t, t), F32),
        scratch_shapes=[pltpu.VMEM((t, t), jnp.int32)],
        compiler_params=_params(("arbitrary", "arbitrary")),
        name="bias_tiles",
    )(rel_bias)


def _bias_decode_body(rb_ref, o_ref):
    c = lax.broadcasted_iota(jnp.int32, (1, 3 * LANES), 1)
    d = jnp.where(c < LANES, PAGE_SIZE - c, jnp.where(c < 2 * LANES, PAST_LEN, 0))
    bucket = _t5_bucket(d)
    for r in range(2 * DIFF_HEADS):
        head = (r // (2 * DIFF_GROUP)) * DIFF_GROUP + r % DIFF_GROUP
        o_ref[r:r + 1, :] = _bias_lookup(bucket, rb_ref, head)


def _bias_decode(rel_bias):
    return pl.pallas_call(
        _bias_decode_body,
        in_specs=[pl.BlockSpec(memory_space=pltpu.SMEM)],
        out_shape=jax.ShapeDtypeStruct((2 * DIFF_HEADS, 3 * LANES), F32),
        name="bias_decode",
    )(rel_bias)


def _diff_proj_body(x_ref, g_ref, w_ref, o_ref, h_ref):
    @pl.when(pl.program_id(1) == 0)
    def _():
        h_ref[...] = _rms(x_ref[...], g_ref[...]).astype(BF16)

    o_ref[...] = _dot(h_ref[...], w_ref[...])


def _diff_proj(x, g, w_in):
    n = x.shape[0]
    tm = min(512, n)
    tn = 512
    nout = w_in.shape[1]
    return pl.pallas_call(
        _diff_proj_body,
        grid=(n // tm, nout // tn),
        in_specs=[
            pl.BlockSpec((tm, D_MODEL), lambda i, j: (i, 0)),
            pl.BlockSpec((1, D_MODEL), lambda i, j: (0, 0)),
            pl.BlockSpec((D_MODEL, tn), lambda i, j: (0, j)),
        ],
        out_specs=pl.BlockSpec((tm, tn), lambda i, j: (i, j)),
        out_shape=jax.ShapeDtypeStruct((n, nout), F32),
        scratch_shapes=[pltpu.VMEM((tm, D_MODEL), BF16)],
        compiler_params=_params(("parallel", "arbitrary")),
        name="diff_proj",
    )(x, g, w_in)


def _lam_full(lam_ref, lam_init):
    lf = lam_ref[...]
    a = jnp.sum(lf[0:1] * lf[1:2], axis=-1, keepdims=True)
    b = jnp.sum(lf[2:3] * lf[3:4], axis=-1, keepdims=True)
    return jnp.exp(a) - jnp.exp(b) + lam_init


def _diff_combine(o0, o1, lam, subln, lam_init):
    a = o0 - lam * o1
    return _rms(a, subln) * (1.0 - lam_init)


def _diff_attn_body(qi_ref, ki_ref, q_ref, k_ref, v_ref, bias_ref, lam_ref, sub_ref, o_ref,
                    q0_ref, q1_ref, m_ref, l_ref, acc_ref, *, lam_init):
    t = pl.program_id(2)
    qi = qi_ref[t]
    ki = ki_ref[t]

    @pl.when(ki == 0)
    def _():
        q = (q_ref[...] * DIFF_SCALE).astype(BF16)
        lane = lax.broadcasted_iota(jnp.int32, q.shape, 1)
        zero = jnp.zeros(q.shape, BF16)
        q0_ref[...] = jnp.where(lane < DIFF_HEAD, q, zero)
        q1_ref[...] = jnp.where(lane < DIFF_HEAD, zero, q)
        m_ref[...] = jnp.full(m_ref.shape, NEG_INF, F32)
        l_ref[...] = jnp.zeros(l_ref.shape, F32)
        acc_ref[...] = jnp.zeros(acc_ref.shape, F32)

    def update(masked):
        k = k_ref[...].astype(BF16)
        v_aug = _with_ones(v_ref[...].astype(BF16))
        for mp, qm_ref in enumerate((q0_ref, q1_ref)):
            _attend_tile(lambda rows, nk: _dot_nt(qm_ref[rows, :], k[:nk]) + bias_ref[rows, :nk],
                         v_aug, m_ref.at[mp], l_ref.at[mp], acc_ref.at[mp], masked, ATTN_ROW_CHUNK)

    @pl.when(ki < qi)
    def _():
        update(False)

    @pl.when(ki == qi)
    def _():
        update(True)
        lam = _lam_full(lam_ref, lam_init)
        o0 = acc_ref[0] / l_ref[0]
        o1 = acc_ref[1] / l_ref[1]
        o_ref[...] = _diff_combine(o0, o1, lam, sub_ref[...], lam_init).astype(BF16)


def _diff_attn(qkv, bias_tiles, lam, subln, lam_init):
    t = ATTN_TILE
    nblk = SEQ // t
    qi, ki = _tri_schedule(nblk)
    kcol0 = DIFF_Q // LANES
    vcol0 = (DIFF_Q + DIFF_K) // LANES
    grid_spec = pltpu.PrefetchScalarGridSpec(
        num_scalar_prefetch=2,
        grid=(BATCH, DIFF_HEADS, qi.shape[0]),
        in_specs=[
            pl.BlockSpec((t, LANES), lambda b, h, s, qi, ki: (b * nblk + qi[s], h)),
            pl.BlockSpec((t, LANES),
                         lambda b, h, s, qi, ki: (b * nblk + ki[s], kcol0 + h // DIFF_GROUP)),
            pl.BlockSpec((t, LANES),
                         lambda b, h, s, qi, ki: (b * nblk + ki[s], vcol0 + h // DIFF_GROUP)),
            pl.BlockSpec((None, None, t, t),
                         lambda b, h, s, qi, ki: (h, jnp.minimum(qi[s] - ki[s], 2), 0, 0)),
            pl.BlockSpec((4, DIFF_HEAD), lambda b, h, s, qi, ki: (0, 0)),
            pl.BlockSpec((1, DIFF_VHEAD), lambda b, h, s, qi, ki: (0, 0)),
        ],
        out_specs=pl.BlockSpec((t, LANES), lambda b, h, s, qi, ki: (b * nblk + qi[s], h)),
        scratch_shapes=[pltpu.VMEM((t, LANES), BF16), pltpu.VMEM((t, LANES), BF16),
                        pltpu.VMEM((2, t, LANES), F32), pltpu.VMEM((2, t, LANES), F32),
                        pltpu.VMEM((2, t, DIFF_VHEAD), F32)],
    )
    return pl.pallas_call(
        functools.partial(_diff_attn_body, lam_init=lam_init),
        grid_spec=grid_spec,
        out_shape=jax.ShapeDtypeStruct((BATCH * SEQ, DIFF_HEADS * DIFF_VHEAD), BF16),
        compiler_params=_params(("parallel", "parallel", "arbitrary")),
        name="diff_attn",
    )(qi, ki, qkv, qkv, qkv, bias_tiles, lam, subln)


def _diff_side(page_table, row0, n_rows, q32, k_new, v_new, dec_bias, lam, subln, cache_kt,
               cache_v2, lam_init):
    nrow = 2 * DIFF_HEADS
    half = nrow // 2
    rows = slice(row0, row0 + n_rows)
    inputs = (q32[rows], k_new[rows].reshape(n_rows, 1, DIFF_K), v_new[rows].reshape(n_rows, 1, DIFF_V),
              dec_bias, lam, subln, cache_kt, cache_v2)
    in_specs = tuple(_resident(a.shape) for a in inputs[:6]) + (
        pl.BlockSpec(memory_space=pl.ANY), pl.BlockSpec(memory_space=pl.ANY))
    scratch = (pltpu.VMEM((RING_SLOTS, PAGES_PER_STEP, DIFF_K, PAGE_SIZE), F32),
               pltpu.VMEM((RING_SLOTS, PAGES_PER_STEP, 2 * PAGE_SIZE, DIFF_VHEAD), F32),
               pltpu.VMEM((DIFF_K, KEYS_PER_STEP), BF16),
               pltpu.VMEM((KEYS_PER_STEP, DIFF_VHEAD), BF16),
               pltpu.VMEM((KEYS_PER_STEP, DIFF_VHEAD), BF16),
               pltpu.VMEM((nrow, DIFF_K), BF16),
               pltpu.VMEM((nrow, LANES), F32), pltpu.VMEM((nrow, LANES), F32),
               pltpu.VMEM((nrow, DIFF_VHEAD), F32),
               pltpu.SemaphoreType.DMA((RING_SLOTS,)))

    def begin(step, pt_ref, in_refs, scratch_refs):
        _side_begin(step, pt_ref, in_refs[6:8], scratch_refs[:2], scratch_refs[-1], row0)

    def run(step, pt_ref, in_refs, o_ref, scratch_refs):
        q_ref, knew_ref, vnew_ref, bias_ref, lam_ref, sub_ref, kt_hbm, v_hbm = in_refs
        kbuf, vbuf, ktb_ref, v0_ref, v1_ref, qbig_ref, m_ref, l_ref, acc_ref, sem = scratch_refs

        def unit(row, sub, i, slot):
            def init():
                r = lax.broadcasted_iota(jnp.int32, (nrow, DIFF_K), 0)
                c = lax.broadcasted_iota(jnp.int32, (nrow, DIFF_K), 1)
                q = q_ref[row] * DIFF_SCALE
                rep = jnp.concatenate([q, q, q, q], axis=-1)
                qbig_ref[...] = jnp.where((c // DIFF_HEAD) == (r // DIFF_GROUP), rep, 0.0).astype(BF16)
                m_ref[...] = jnp.full(m_ref.shape, NEG_INF, F32)
                l_ref[...] = jnp.zeros(l_ref.shape, F32)
                acc_ref[...] = jnp.zeros(acc_ref.shape, F32)

            _when_first_group(sub, i, init)
            for k in range(PAGES_PER_STEP):
                keys = slice(k * PAGE_SIZE, (k + 1) * PAGE_SIZE)
                ktb_ref[:, keys] = kbuf[slot, k].astype(BF16)
                v0_ref[keys, :] = vbuf[slot, k, pl.ds(0, PAGE_SIZE, stride=2), :].astype(BF16)
                v1_ref[keys, :] = vbuf[slot, k, pl.ds(1, PAGE_SIZE, stride=2), :].astype(BF16)
            qbig = qbig_ref[...]
            far = bias_ref[:, LANES:2 * LANES]
            s = _dot(qbig, ktb_ref[...]) + _lane_tile(far, KEYS_PER_STEP)
            if i == UNITS_PER_STEP - 1:
                near_delta = bias_ref[:, :LANES] - far
                tail = (s[:, KEYS_PER_STEP - PAGE_SIZE:]
                        + jnp.where(sub == STEPS_PER_ROW - 1, near_delta, 0.0))
                s = jnp.concatenate([s[:, :KEYS_PER_STEP - PAGE_SIZE], tail], axis=1)
            m, corr, p, l = _softmax_step(s, m_ref[...], l_ref[...])
            pb = p.astype(BF16)
            pv = jnp.concatenate([_dot(pb[:half], v0_ref[...]), _dot(pb[half:], v1_ref[...])], axis=0)
            acc = acc_ref[...] * corr + pv
            m_ref[...] = m
            l_ref[...] = l
            acc_ref[...] = acc

            def finish():
                kn = knew_ref[row].astype(BF16).astype(F32)
                vn = vnew_ref[row].astype(BF16).astype(F32)
                s_new = (jnp.sum(qbig.astype(F32) * kn, axis=-1, keepdims=True)
                         + bias_ref[:, 2 * LANES:])
                m_new = jnp.maximum(m, s_new)
                c2 = jnp.exp(m - m_new)
                p_new = jnp.exp(s_new - m_new)
                l2 = l * c2 + p_new
                rv = lax.broadcasted_iota(jnp.int32, (nrow, DIFF_VHEAD), 0)
                v_row = jnp.where(rv < half, vn[:, :DIFF_VHEAD], vn[:, DIFF_VHEAD:])
                o = (acc * c2 + p_new.astype(BF16).astype(F32) * v_row) / l2
                g8 = DIFF_GROUP
                o0 = jnp.concatenate([o[0:g8], o[2 * g8:3 * g8]], axis=0)
                o1 = jnp.concatenate([o[g8:2 * g8], o[3 * g8:4 * g8]], axis=0)
                lam_full = _lam_full(lam_ref, lam_init)
                o_ref[row] = _diff_combine(o0, o1, lam_full, sub_ref[...], lam_init).astype(BF16)

            _when_last_group(sub, i, finish)

        _side_units(step, pt_ref, (kt_hbm, v_hbm), (kbuf, vbuf), sem, row0, n_rows, unit)

    return _Side(page_table.reshape(-1), inputs, in_specs,
                 jax.ShapeDtypeStruct((n_rows, DIFF_HEADS, DIFF_VHEAD), BF16), scratch, begin, run)


def _rot_half_cols(w):
    half = w.shape[-1] // 2
    return jnp.concatenate([-w[..., half:], w[..., :half]], axis=-1)


def _pad_cols(w, width):
    return jnp.pad(w, [(0, 0)] * (w.ndim - 1) + [(0, width - w.shape[-1])])


def _prep_mla(w_in, q_norm, kv_norm, w_q_up, w_kv_up, w_out):
    w_kr = w_in[:, Q_LORA + KV_LORA:]
    w_in2 = jnp.concatenate([w_in[:, :Q_LORA + KV_LORA], _pad_cols(w_kr, LANES),
                             _pad_cols(_rot_half_cols(w_kr), LANES)], axis=1)
    wq = w_q_up.reshape(Q_LORA, MLA_HEADS, QK_NOPE + QK_ROPE)
    wr = wq[:, :, QK_NOPE:]
    hw = MLA_HEADS * LANES
    wkv = w_kv_up.reshape(KV_LORA, MLA_HEADS, QK_NOPE + V_HEAD)
    return {
        "w_in": w_in2.astype(BF16),
        "q_norm": q_norm.reshape(1, Q_LORA),
        "kv_norm": kv_norm.reshape(1, KV_LORA),
        "w_qn": wq[:, :, :QK_NOPE].reshape(Q_LORA, hw).astype(BF16),
        "w_qa": _pad_cols(wr, LANES).reshape(Q_LORA, hw).astype(BF16),
        "w_qb": _pad_cols(_rot_half_cols(wr), LANES).reshape(Q_LORA, hw).astype(BF16),
        "w_kv": w_kv_up.astype(BF16),
        "w_ukt": jnp.transpose(wkv[:, :, :QK_NOPE], (1, 2, 0)).astype(BF16),
        "w_uv": jnp.transpose(wkv[:, :, QK_NOPE:], (1, 0, 2)).astype(BF16),
        "w_out": w_out.astype(BF16),
    }


def _rope_tables(pos):
    inv = ROPE_THETA ** (-jnp.arange(0, QK_ROPE, 2, dtype=F32) / QK_ROPE)
    ang = pos.astype(F32)[:, None] * inv[None, :]
    reps = LANES // (QK_ROPE // 2)
    return jnp.tile(jnp.cos(ang), (1, reps)), jnp.tile(jnp.sin(ang), (1, reps))


def kernel(x_prompt, x_sample, cache_mla_ckv, cache_mla_krope, cache_diff_k, cache_diff_v, page_table, ffn_norm, w_ffn_up, w_ffn_down, mix_norm, mla_w_in, mla_q_norm, mla_kv_norm, mla_w_q_up, mla_w_kv_up, mla_w_out, diff_w_in, diff_lambda, diff_subln, diff_w_out, rel_bias, final_norm):
    xp = x_prompt.reshape(BATCH * SEQ, D_MODEL)
    xs = x_sample.reshape(DEC_BATCH, D_MODEL)
    w_up = w_ffn_up.astype(BF16)
    w_down = w_ffn_down.astype(BF16)
    gf = final_norm.reshape(1, D_MODEL)
    cos_p, sin_p = _rope_tables(jnp.arange(SEQ, dtype=jnp.int32))
    cos_s, sin_s = _rope_tables(jnp.full((1,), PAST_LEN, jnp.int32))
    n_phys = cache_mla_ckv.shape[1]

    def ffn(x, i, k, final=False, side=None):
        return _ffn(x, ffn_norm[i, k].reshape(1, D_MODEL), w_up, w_down, gf, i, k, final=final,
                    side=side)

    cache_krt = jnp.transpose(cache_mla_krope, (0, 1, 3, 2))
    cache_kt = jnp.transpose(cache_diff_k, (0, 1, 3, 4, 5, 2)).reshape(1, n_phys, DIFF_K, PAGE_SIZE)
    cache_v2 = cache_diff_v.reshape(1, n_phys, 2 * PAGE_SIZE, DIFF_VHEAD)
    half = DEC_BATCH // 2

    xs = ffn(xs, 0, 0)
    g0 = mix_norm[0].reshape(1, D_MODEL)
    wm = _prep_mla(mla_w_in[0], mla_q_norm[0], mla_kv_norm[0], mla_w_q_up[0], mla_w_kv_up[0],
                   mla_w_out[0])
    ckv_s, kr_s, qn_s, qr_s, _, _ = _mla_proj(xs, g0, wm, cos_s, sin_s)
    q_dec = jnp.transpose(_mla_qlat(qn_s, qr_s, wm["w_ukt"]), (1, 0, 2))

    def mla_side(row0):
        return _mla_side(page_table, row0, half, q_dec, ckv_s, kr_s, cache_mla_ckv, cache_krt)

    xp, o_lat_a = ffn(xp, 0, 0, side=mla_side(0))
    ckv_p, kr_p, qn_p, qr_p, kv_p, krb_p = _mla_proj(xp, g0, wm, cos_p, sin_p)
    o_p = _mla_attn(qn_p, qr_p, kv_p, krb_p)
    xp = _outproj(xp, o_p, wm["w_out"])
    xp, o_lat_b = ffn(xp, 0, 1, side=mla_side(half))

    o_lat = jnp.concatenate([o_lat_a, o_lat_b], axis=0)
    xs = _mla_finish(xs, jnp.transpose(o_lat, (1, 0, 2)), wm["w_uv"], wm["w_out"])
    xs = ffn(xs, 0, 1)

    xs = ffn(xs, 1, 0)
    lam_init = 0.8 - 0.6 * math.exp(-0.3 * 1)
    g1 = mix_norm[1].reshape(1, D_MODEL)
    w_din = diff_w_in[0].astype(BF16)
    w_dout = diff_w_out[0].astype(BF16)
    lam = diff_lambda[0]
    subln = diff_subln[0].reshape(1, DIFF_VHEAD)
    bias_tiles = _bias_tiles(rel_bias)
    dec_bias = _bias_decode(rel_bias)

    qkv_s = _diff_proj(xs, g1, w_din)
    q32 = qkv_s[:, :DIFF_Q].reshape(DEC_BATCH, DIFF_KV_HEADS, DIFF_GROUP, 2, DIFF_HEAD)
    q32 = jnp.transpose(q32, (0, 1, 3, 2, 4)).reshape(DEC_BATCH, 2 * DIFF_HEADS, DIFF_HEAD)
    k_s = qkv_s[:, DIFF_Q:DIFF_Q + DIFF_K]
    v_s = qkv_s[:, DIFF_Q + DIFF_K:]

    def diff_side(row0):
        return _diff_side(page_table, row0, half, q32, k_s, v_s, dec_bias, lam, subln, cache_kt,
                          cache_v2, lam_init)

    xp, a_s_a = ffn(xp, 1, 0, side=diff_side(0))
    qkv_p = _diff_proj(xp, g1, w_din)
    a_p = _diff_attn(qkv_p, bias_tiles, lam, subln, lam_init)
    xp = _outproj(xp, a_p, w_dout)
    xp, a_s_b = ffn(xp, 1, 1, final=True, side=diff_side(half))

    a_s = jnp.concatenate([a_s_a, a_s_b], axis=0)
    xs = _outproj(xs, a_s.reshape(DEC_BATCH, DIFF_HEADS * DIFF_VHEAD), w_dout)
    xs = ffn(xs, 1, 1, final=True)

    k_p = qkv_p[:, DIFF_Q:DIFF_Q + DIFF_K]
    v_p = qkv_p[:, DIFF_Q + DIFF_K:]
    return (
        xp.reshape(BATCH, SEQ, D_MODEL),
        xs.reshape(DEC_BATCH, 1, D_MODEL),
        ckv_p.reshape(1, BATCH, SEQ, KV_LORA),
        kr_p.reshape(1, BATCH, SEQ, QK_ROPE),
        k_p.reshape(1, BATCH, SEQ, DIFF_KV_HEADS, 2, DIFF_HEAD),
        v_p.reshape(1, BATCH, SEQ, DIFF_KV_HEADS, DIFF_VHEAD),
        ckv_s.reshape(1, DEC_BATCH, 1, KV_LORA),
        kr_s.reshape(1, DEC_BATCH, 1, QK_ROPE),
        k_s.reshape(1, DEC_BATCH, 1, DIFF_KV_HEADS, 2, DIFF_HEAD),
        v_s.reshape(1, DEC_BATCH, 1, DIFF_KV_HEADS, DIFF_VHEAD),
    )
```

```python
import functools
import math
from typing import Any, Callable, NamedTuple

import jax
import jax.numpy as jnp
import numpy as np
from jax import lax
from jax.experimental import pallas as pl
from jax.experimental.pallas import tpu as pltpu

D_MODEL = 2048
BATCH = 4
SEQ = 2048
DEPTH = 2
DEC_BATCH = 128
PAST_LEN = 16384
PAGE_SIZE = 128
N_PAGES = PAST_LEN // PAGE_SIZE
D_FF = 5632
EPS = 1e-6
MLA_HEADS = 16
Q_LORA = 512
KV_LORA = 512
QK_NOPE = 128
QK_ROPE = 64
V_HEAD = 128
MLA_SCALE = (QK_NOPE + QK_ROPE) ** -0.5
ROPE_THETA = 10000.0
DIFF_HEADS = 16
DIFF_KV_HEADS = 2
DIFF_GROUP = DIFF_HEADS // DIFF_KV_HEADS
DIFF_HEAD = 64
DIFF_VHEAD = 128
DIFF_Q = DIFF_HEADS * 2 * DIFF_HEAD
DIFF_K = DIFF_KV_HEADS * 2 * DIFF_HEAD
DIFF_V = DIFF_KV_HEADS * DIFF_VHEAD
DIFF_SCALE = DIFF_HEAD ** -0.5
NUM_BUCKETS = 32
MAX_DISTANCE = 128
NEG_INF = -1e30

LANES = 128
V7X_VMEM_LIMIT_BYTES = 60000 * 1024

FFN_ROW_TILE = 512
FFN_FF_TILE = 512
PROJ_ROW_TILE = 256
ATTN_TILE = 512
ATTN_ROW_CHUNK = 256
PAGES_PER_STEP = 16
KEYS_PER_STEP = PAGES_PER_STEP * PAGE_SIZE
PAGE_GROUPS = N_PAGES // PAGES_PER_STEP
UNITS_PER_STEP = 4
STEPS_PER_ROW = PAGE_GROUPS // UNITS_PER_STEP
RING_SLOTS = 4
PREFETCH = RING_SLOTS - 1
assert UNITS_PER_STEP % RING_SLOTS == 0
assert math.frexp(DIFF_SCALE)[0] == 0.5
BF16 = jnp.bfloat16
F32 = jnp.float32


def _params(sem):
    return pltpu.CompilerParams(dimension_semantics=sem, vmem_limit_bytes=V7X_VMEM_LIMIT_BYTES)


def _rms(x, g):
    return x * lax.rsqrt(jnp.mean(x * x, axis=-1, keepdims=True) + EPS) * g


def _dot(a, b):
    return jnp.dot(a, b, preferred_element_type=F32)


def _dot_nt(a, b):
    return lax.dot_general(a, b, (((1,), (1,)), ((), ())), preferred_element_type=F32)


def _resident(shape):
    nd = len(shape)
    return pl.BlockSpec(shape, lambda *_: (0,) * nd, pipeline_mode=pl.Buffered(1))


class _Side(NamedTuple):
    page_table: Any
    inputs: tuple
    in_specs: tuple
    out_shape: Any
    scratch: tuple
    begin: Callable
    run: Callable


def _ffn_body(*refs, final, side):
    n_side_in = len(side.inputs) if side else 0
    pt_ref = refs[0] if side else None
    refs = refs[1:] if side else refs
    x_ref, g_ref, wg_ref, wu_ref, wd_ref, gf_ref = refs[:6]
    side_in = refs[6:6 + n_side_in]
    o_ref = refs[6 + n_side_in]
    rest = refs[7 + n_side_in:]
    side_out, rest = (rest[0], rest[1:]) if side else (None, rest)
    h_ref, side_scratch = rest[0], rest[1:]
    j = pl.program_id(1)
    step = pl.program_id(0) * pl.num_programs(1) + j

    if side:
        side.begin(step, pt_ref, side_in, side_scratch)

    @pl.when(j == 0)
    def _():
        x = x_ref[...]
        h_ref[...] = _rms(x, g_ref[...]).astype(BF16)
        o_ref[...] = x

    h = h_ref[...]
    gate = _dot(h, wg_ref[...])
    up = _dot(h, wu_ref[...])
    a = (gate / (1.0 + jnp.exp(-gate)) * up).astype(BF16)
    o_ref[...] += 0.5 * _dot(a, wd_ref[...])

    if final:
        @pl.when(j == pl.num_programs(1) - 1)
        def _():
            o_ref[...] = _rms(o_ref[...], gf_ref[...])

    if side:
        side.run(step, pt_ref, side_in, side_out, side_scratch)


def _ffn(x, g, w_up, w_down, gf, layer, pos, *, final, side=None):
    n = x.shape[0]
    tm = min(FFN_ROW_TILE, n)
    tf = FFN_FF_TILE
    nf = D_FF // tf
    in_specs = [
        pl.BlockSpec((tm, D_MODEL), lambda i, j, *_: (i, 0)),
        pl.BlockSpec((1, D_MODEL), lambda i, j, *_: (0, 0)),
        pl.BlockSpec((None, None, D_MODEL, tf), lambda i, j, *_: (layer, pos, 0, j)),
        pl.BlockSpec((None, None, D_MODEL, tf), lambda i, j, *_: (layer, pos, 0, j + nf)),
        pl.BlockSpec((None, None, tf, D_MODEL), lambda i, j, *_: (layer, pos, j, 0)),
        pl.BlockSpec((1, D_MODEL), lambda i, j, *_: (0, 0)),
    ]
    out_specs = pl.BlockSpec((tm, D_MODEL), lambda i, j, *_: (i, 0))
    out_shape = jax.ShapeDtypeStruct((n, D_MODEL), F32)
    scratch = [pltpu.VMEM((tm, D_MODEL), BF16)]
    operands = (x, g, w_up, w_up, w_down, gf)
    if side:
        assert (n // tm) * nf >= side.out_shape.shape[0] * PAGE_GROUPS // UNITS_PER_STEP
        in_specs += list(side.in_specs)
        nd_side = len(side.out_shape.shape)
        out_specs = [out_specs, pl.BlockSpec(side.out_shape.shape, lambda *_: (0,) * nd_side)]
        out_shape = [out_shape, side.out_shape]
        scratch += list(side.scratch)
        operands = (side.page_table,) + operands + tuple(side.inputs)
    grid_spec = pltpu.PrefetchScalarGridSpec(
        num_scalar_prefetch=1 if side else 0, grid=(n // tm, nf),
        in_specs=in_specs, out_specs=out_specs, scratch_shapes=scratch)
    sem = ("arbitrary", "arbitrary") if side else ("parallel", "arbitrary")
    return pl.pallas_call(
        functools.partial(_ffn_body, final=final, side=side),
        grid_spec=grid_spec,
        out_shape=out_shape,
        compiler_params=_params(sem),
        name="ffn_decode" if side else "ffn",
    )(*operands)


def _outproj_body(x_ref, a_ref, w_ref, o_ref):
    o_ref[...] = x_ref[...] + _dot(a_ref[...], w_ref[...])


def _outproj(x, a, w):
    n = x.shape[0]
    tm = min(512, n)
    return pl.pallas_call(
        _outproj_body,
        grid=(n // tm,),
        in_specs=[
            pl.BlockSpec((tm, D_MODEL), lambda i: (i, 0)),
            pl.BlockSpec((tm, a.shape[1]), lambda i: (i, 0)),
            _resident(w.shape),
        ],
        out_specs=pl.BlockSpec((tm, D_MODEL), lambda i: (i, 0)),
        out_shape=jax.ShapeDtypeStruct((n, D_MODEL), F32),
        compiler_params=_params(("parallel",)),
        name="outproj",
    )(x, a, w)


def _mla_proj_body(x_ref, g_ref, win_ref, qn_ref, kvn_ref, cos_ref, sin_ref,
                   wqn_ref, wqa_ref, wqb_ref, wkv_ref,
                   ckv_ref, kr_ref, qnope_ref, qrope_ref, kv_ref, krb_ref):
    h = _rms(x_ref[...], g_ref[...]).astype(BF16)
    proj = _dot(h, win_ref[...])
    q_c = _rms(proj[:, :Q_LORA], qn_ref[...]).astype(BF16)
    ckv = _rms(proj[:, Q_LORA:Q_LORA + KV_LORA], kvn_ref[...])
    cos = cos_ref[...]
    sin = sin_ref[...]
    kr = proj[:, 1024:1152] * cos + proj[:, 1152:1280] * sin
    ckv_ref[...] = ckv
    kr_ref[...] = kr[:, :QK_ROPE]
    ckv_b = ckv.astype(BF16)
    krb_ref[...] = kr.astype(BF16)
    qnope_ref[...] = _dot(q_c, wqn_ref[...]).astype(BF16)
    qa = _dot(q_c, wqa_ref[...])
    qb = _dot(q_c, wqb_ref[...])
    for hd in range(MLA_HEADS):
        sl = slice(hd * LANES, (hd + 1) * LANES)
        qrope_ref[:, sl] = (qa[:, sl] * cos + qb[:, sl] * sin).astype(BF16)
    kv_ref[...] = _dot(ckv_b, wkv_ref[...]).astype(BF16)


def _mla_proj(x, g, w, cos_t, sin_t):
    n = x.shape[0]
    tm = min(PROJ_ROW_TILE, n)
    tt = cos_t.shape[0]
    tb = tm if tt > 1 else 1
    nt = tt // tb
    row = lambda i: (i, 0)
    hw = MLA_HEADS * LANES
    outs = [
        jax.ShapeDtypeStruct((n, KV_LORA), F32),
        jax.ShapeDtypeStruct((n, QK_ROPE), F32),
        jax.ShapeDtypeStruct((n, hw), BF16),
        jax.ShapeDtypeStruct((n, hw), BF16),
        jax.ShapeDtypeStruct((n, 2 * hw), BF16),
        jax.ShapeDtypeStruct((n, LANES), BF16),
    ]
    return pl.pallas_call(
        _mla_proj_body,
        grid=(n // tm,),
        in_specs=[
            pl.BlockSpec((tm, D_MODEL), row),
            _resident((1, D_MODEL)),
            _resident(w["w_in"].shape),
            _resident((1, Q_LORA)),
            _resident((1, KV_LORA)),
            pl.BlockSpec((tb, LANES), lambda i: (i % nt, 0)),
            pl.BlockSpec((tb, LANES), lambda i: (i % nt, 0)),
            _resident(w["w_qn"].shape),
            _resident(w["w_qa"].shape),
            _resident(w["w_qb"].shape),
            _resident(w["w_kv"].shape),
        ],
        out_specs=[pl.BlockSpec((tm, s.shape[1]), row) for s in outs],
        out_shape=outs,
        compiler_params=_params(("parallel",)),
        name="mla_proj",
    )(x, g, w["w_in"], w["q_norm"], w["kv_norm"], cos_t, sin_t,
      w["w_qn"], w["w_qa"], w["w_qb"], w["w_kv"])


def _tri_schedule(nblk):
    qi, ki = [], []
    for q in range(nblk):
        for k in range(q + 1):
            qi.append(q)
            ki.append(k)
    return jnp.asarray(qi, jnp.int32), jnp.asarray(ki, jnp.int32)


def _lane_tile(x, width):
    return jnp.concatenate([x] * (width // LANES), axis=1)


def _with_ones(v):
    return jnp.concatenate([v, jnp.ones(v.shape, BF16)], axis=1)


def _online(s, v_aug, m_ref, l_ref, acc_ref):
    m_old = m_ref[...]
    m_new = jnp.maximum(m_old, jnp.max(s, axis=-1, keepdims=True))
    corr = jnp.exp(m_old - m_new)
    p = jnp.exp(s - _lane_tile(m_new, s.shape[1]))
    pv = _dot(p.astype(BF16), v_aug)
    l_ref[...] = l_ref[...] * corr + pv[:, LANES:]
    acc_ref[...] = acc_ref[...] * corr + pv[:, :LANES]
    m_ref[...] = m_new


def _attend_tile(score_fn, v_aug, m_ref, l_ref, acc_ref, masked, chunk):
    t = v_aug.shape[0]
    for r0 in range(0, t, chunk):
        nk = r0 + chunk if masked else t
        rows = slice(r0, r0 + chunk)
        s = score_fn(rows, nk)
        if masked:
            r = lax.broadcasted_iota(jnp.int32, s.shape, 0) + r0
            c = lax.broadcasted_iota(jnp.int32, s.shape, 1)
            s = jnp.where(c <= r, s, NEG_INF)
        _online(s, v_aug[:nk], m_ref.at[rows], l_ref.at[rows], acc_ref.at[rows])


def _mla_attn_body(qi_ref, ki_ref, qn_ref, qr_ref, kn_ref, kr_ref, v_ref, o_ref,
                   q_ref, m_ref, l_ref, acc_ref):
    t = pl.program_id(2)
    qi = qi_ref[t]
    ki = ki_ref[t]

    @pl.when(ki == 0)
    def _():
        q_ref[...] = jnp.concatenate([qn_ref[...], qr_ref[...]], axis=1)
        m_ref[...] = jnp.full(m_ref.shape, NEG_INF, F32)
        l_ref[...] = jnp.zeros(l_ref.shape, F32)
        acc_ref[...] = jnp.zeros(acc_ref.shape, F32)

    k = jnp.concatenate([kn_ref[...], kr_ref[...]], axis=1)
    s = _dot_nt(q_ref[...], k) * MLA_SCALE
    v_aug = _with_ones(v_ref[...])

    def update(masked):
        _attend_tile(lambda rows, nk: s[rows, :nk], v_aug, m_ref, l_ref, acc_ref, masked, ATTN_TILE)

    @pl.when(ki < qi)
    def _():
        update(False)

    @pl.when(ki == qi)
    def _():
        update(True)
        o_ref[...] = (acc_ref[...] / l_ref[...]).astype(BF16)


def _mla_attn(q_nope, q_rope, kv, kr_b):
    t = ATTN_TILE
    nblk = SEQ // t
    qi, ki = _tri_schedule(nblk)
    qmap = lambda b, h, s, qi, ki: (b * nblk + qi[s], h)
    grid_spec = pltpu.PrefetchScalarGridSpec(
        num_scalar_prefetch=2,
        grid=(BATCH, MLA_HEADS, qi.shape[0]),
        in_specs=[
            pl.BlockSpec((t, LANES), qmap),
            pl.BlockSpec((t, LANES), qmap),
            pl.BlockSpec((t, LANES), lambda b, h, s, qi, ki: (b * nblk + ki[s], 2 * h)),
            pl.BlockSpec((t, LANES), lambda b, h, s, qi, ki: (b * nblk + ki[s], 0)),
            pl.BlockSpec((t, LANES), lambda b, h, s, qi, ki: (b * nblk + ki[s], 2 * h + 1)),
        ],
        out_specs=pl.BlockSpec((t, LANES), qmap),
        scratch_shapes=[pltpu.VMEM((t, 2 * LANES), BF16),
                        pltpu.VMEM((t, LANES), F32), pltpu.VMEM((t, LANES), F32),
                        pltpu.VMEM((t, V_HEAD), F32)],
    )
    return pl.pallas_call(
        _mla_attn_body,
        grid_spec=grid_spec,
        out_shape=jax.ShapeDtypeStruct((BATCH * SEQ, MLA_HEADS * V_HEAD), BF16),
        compiler_params=_params(("parallel", "parallel", "arbitrary")),
        name="mla_attn",
    )(qi, ki, q_nope, q_rope, kv, kr_b, kv)


def _mla_qlat_body(qn_ref, qr_ref, wukt_ref, o_ref):
    for hd in range(MLA_HEADS):
        ql = _dot(qn_ref[:, hd * LANES:(hd + 1) * LANES], wukt_ref[hd])
        o_ref[hd, :, :KV_LORA] = ql.astype(BF16)
        o_ref[hd, :, KV_LORA:] = qr_ref[:, hd * LANES:(hd + 1) * LANES]


def _mla_qlat(q_nope, q_rope, w_ukt):
    return pl.pallas_call(
        _mla_qlat_body,
        out_shape=jax.ShapeDtypeStruct((MLA_HEADS, DEC_BATCH, KV_LORA + LANES), BF16),
        compiler_params=_params(None),
        name="mla_qlat",
    )(q_nope, q_rope, w_ukt)


def _group_copies(page_of, hbm_refs, bufs, sem, slot):
    copies = []
    for k in range(PAGES_PER_STEP):
        page = page_of(k)
        for hbm, buf in zip(hbm_refs, bufs):
            copies.append(pltpu.make_async_copy(hbm.at[0, page], buf.at[slot, k], sem.at[slot]))
    return copies


def _start_group(pt_ref, base, hbm_refs, bufs, sem, slot):
    copies = _group_copies(lambda k: pt_ref[base + k], hbm_refs, bufs, sem, slot)
    for i, c in enumerate(copies):
        c.start(priority=(i // len(hbm_refs)) % 2)


def _wait_group(hbm_refs, bufs, sem, slot):
    for c in _group_copies(lambda k: 0, hbm_refs, bufs, sem, slot):
        c.wait()


def _side_begin(step, pt_ref, hbm_refs, bufs, sem, row0):
    @pl.when(step == 0)
    def _():
        for d in range(PREFETCH):
            _start_group(pt_ref, row0 * N_PAGES + d * PAGES_PER_STEP, hbm_refs, bufs, sem, d)


def _side_units(step, pt_ref, hbm_refs, bufs, sem, row0, n_rows, unit_fn):
    n_steps = n_rows * STEPS_PER_ROW
    first_base = row0 * N_PAGES
    last_base = (row0 + n_rows) * N_PAGES - PAGES_PER_STEP

    @pl.when(step < n_steps)
    def _():
        for i in range(UNITS_PER_STEP):
            slot = i % RING_SLOTS
            nxt = first_base + (step * UNITS_PER_STEP + i + PREFETCH) * PAGES_PER_STEP
            _start_group(pt_ref, jnp.minimum(nxt, last_base), hbm_refs, bufs, sem,
                         (i + PREFETCH) % RING_SLOTS)
            _wait_group(hbm_refs, bufs, sem, slot)
            unit_fn(step // STEPS_PER_ROW, step % STEPS_PER_ROW, i, slot)

    @pl.when(step == n_steps - 1)
    def _():
        for d in range(PREFETCH):
            _wait_group(hbm_refs, bufs, sem, d)


def _when_first_group(sub, i, fn):
    if i == 0:
        pl.when(sub == 0)(fn)


def _when_last_group(sub, i, fn):
    if i == UNITS_PER_STEP - 1:
        pl.when(sub == STEPS_PER_ROW - 1)(fn)


def _softmax_step(s, m, l):
    m_new = jnp.maximum(m, jnp.max(s, axis=-1, keepdims=True))
    corr = jnp.exp(m - m_new)
    p = jnp.exp(s - _lane_tile(m_new, s.shape[1]))
    return m_new, corr, p, l * corr + jnp.sum(p, axis=-1, keepdims=True)


def _mla_side(page_table, row0, n_rows, q_dec, ckv_new, kr_new, cache_ckv, cache_krt):
    rows = slice(row0, row0 + n_rows)
    inputs = (q_dec[rows], ckv_new[rows].reshape(n_rows, 1, KV_LORA),
              kr_new[rows].reshape(n_rows, 1, QK_ROPE), cache_ckv, cache_krt)
    in_specs = (_resident(inputs[0].shape), _resident(inputs[1].shape), _resident(inputs[2].shape),
                pl.BlockSpec(memory_space=pl.ANY), pl.BlockSpec(memory_space=pl.ANY))
    scratch = (pltpu.VMEM((RING_SLOTS, PAGES_PER_STEP, PAGE_SIZE, KV_LORA), F32),
               pltpu.VMEM((RING_SLOTS, PAGES_PER_STEP, QK_ROPE, PAGE_SIZE), F32),
               pltpu.VMEM((KEYS_PER_STEP, KV_LORA), BF16),
               pltpu.VMEM((QK_ROPE, KEYS_PER_STEP), BF16),
               pltpu.VMEM((MLA_HEADS, LANES), F32), pltpu.VMEM((MLA_HEADS, LANES), F32),
               pltpu.VMEM((MLA_HEADS, KV_LORA), F32),
               pltpu.SemaphoreType.DMA((RING_SLOTS,)))

    def begin(step, pt_ref, in_refs, scratch_refs):
        cbuf, rbuf = scratch_refs[:2]
        _side_begin(step, pt_ref, in_refs[3:5], (cbuf, rbuf), scratch_refs[-1], row0)

    def run(step, pt_ref, in_refs, o_ref, scratch_refs):
        q_ref, cnew_ref, krnew_ref, ckv_hbm, krt_hbm = in_refs
        cbuf, rbuf, kc_ref, krt_ref, m_ref, l_ref, acc_ref, sem = scratch_refs

        def unit(row, sub, i, slot):
            def init():
                m_ref[...] = jnp.full(m_ref.shape, NEG_INF, F32)
                l_ref[...] = jnp.zeros(l_ref.shape, F32)
                acc_ref[...] = jnp.zeros(acc_ref.shape, F32)

            _when_first_group(sub, i, init)
            q = q_ref[row]
            q_lat = q[:, :KV_LORA]
            q_rope = q[:, KV_LORA:KV_LORA + QK_ROPE]
            for k in range(PAGES_PER_STEP):
                keys = slice(k * PAGE_SIZE, (k + 1) * PAGE_SIZE)
                kc_ref[keys, :] = cbuf[slot, k].astype(BF16)
                krt_ref[:, keys] = rbuf[slot, k].astype(BF16)
            kc = kc_ref[...]
            s = (_dot_nt(q_lat, kc) + _dot(q_rope, krt_ref[...])) * MLA_SCALE
            m, corr, p, l = _softmax_step(s, m_ref[...], l_ref[...])
            acc = acc_ref[...] * _lane_tile(corr, KV_LORA) + _dot(p.astype(BF16), kc)
            m_ref[...] = m
            l_ref[...] = l
            acc_ref[...] = acc

            def finish():
                cnew = cnew_ref[row].astype(BF16).astype(F32)
                krnew = krnew_ref[row].astype(BF16).astype(F32)
                s_new = (jnp.sum(q_lat.astype(F32) * cnew, axis=-1, keepdims=True)
                         + jnp.sum(q_rope.astype(F32) * krnew, axis=-1, keepdims=True)) * MLA_SCALE
                m_new = jnp.maximum(m, s_new)
                c2 = jnp.exp(m - m_new)
                p_new = jnp.exp(s_new - m_new)
                l2 = l * c2 + p_new
                acc2 = (acc * _lane_tile(c2, KV_LORA)
                        + _lane_tile(p_new.astype(BF16).astype(F32), KV_LORA) * cnew)
                o_ref[row] = acc2 / _lane_tile(l2, KV_LORA)

            _when_last_group(sub, i, finish)

        _side_units(step, pt_ref, (ckv_hbm, krt_hbm), (cbuf, rbuf), sem, row0, n_rows, unit)

    return _Side(page_table.reshape(-1), inputs, in_specs,
                 jax.ShapeDtypeStruct((n_rows, MLA_HEADS, KV_LORA), F32), scratch, begin, run)


def _mla_finish_body(x_ref, olat_ref, wuv_ref, wout_ref, o_ref, o_scr):
    for hd in range(MLA_HEADS):
        o_scr[:, hd * V_HEAD:(hd + 1) * V_HEAD] = _dot(
            olat_ref[hd].astype(BF16), wuv_ref[hd]).astype(BF16)
    o_ref[...] = x_ref[...] + _dot(o_scr[...], wout_ref[...])


def _mla_finish(x, o_lat, w_uv, w_out):
    return pl.pallas_call(
        _mla_finish_body,
        out_shape=jax.ShapeDtypeStruct((DEC_BATCH, D_MODEL), F32),
        scratch_shapes=[pltpu.VMEM((DEC_BATCH, MLA_HEADS * V_HEAD), BF16)],
        compiler_params=_params(None),
        name="mla_finish",
    )(x, o_lat, w_uv, w_out)


def _t5_bucket(d):
    max_exact = NUM_BUCKETS // 2
    nf = jnp.maximum(d, 1).astype(F32)
    large = max_exact + (jnp.log(nf / max_exact) / math.log(MAX_DISTANCE / max_exact)
                         * (NUM_BUCKETS - max_exact)).astype(jnp.int32)
    return jnp.where(d < max_exact, d, jnp.minimum(large, NUM_BUCKETS - 1))


def _bias_lookup(bucket, rb_ref, h):
    out = jnp.zeros(bucket.shape, F32)
    for b in range(NUM_BUCKETS):
        out = jnp.where(bucket == b, rb_ref[b, h], out)
    return out


def _bias_tiles_body(rb_ref, o_ref, bk_ref):
    cls = pl.program_id(0)
    h = pl.program_id(1)

    @pl.when(h == 0)
    def _():
        r = lax.broadcasted_iota(jnp.int32, bk_ref.shape, 0)
        c = lax.broadcasted_iota(jnp.int32, bk_ref.shape, 1)
        d = jnp.maximum(cls * ATTN_TILE + r - c, 0)
        bk_ref[...] = _t5_bucket(d)

    o_ref[...] = _bias_lookup(bk_ref[...], rb_ref, h)


def _bias_tiles(rel_bias):
    t = ATTN_TILE
    return pl.pallas_call(
        _bias_tiles_body,
        grid=(3, DIFF_HEADS),
        in_specs=[pl.BlockSpec(memory_space=pltpu.SMEM)],
        out_specs=pl.BlockSpec((None, None, t, t), lambda c, h: (h, c, 0, 0)),
        out_shape=jax.ShapeDtypeStruct((DIFF_HEADS, 3, t, t), F32),
        scratch_shapes=[pltpu.VMEM((t, t), jnp.int32)],
        compiler_params=_params(("arbitrary", "arbitrary")),
        name="bias_tiles",
    )(rel_bias)


def _bias_decode_body(rb_ref, o_ref):
    c = lax.broadcasted_iota(jnp.int32, (1, 3 * LANES), 1)
    d = jnp.where(c < LANES, PAGE_SIZE - c, jnp.where(c < 2 * LANES, PAST_LEN, 0))
    bucket = _t5_bucket(d)
    for r in range(2 * DIFF_HEADS):
        head = (r // (2 * DIFF_GROUP)) * DIFF_GROUP + r % DIFF_GROUP
        o_ref[r:r + 1, :] = _bias_lookup(bucket, rb_ref, head)


def _bias_decode(rel_bias):
    return pl.pallas_call(
        _bias_decode_body,
        in_specs=[pl.BlockSpec(memory_space=pltpu.SMEM)],
        out_shape=jax.ShapeDtypeStruct((2 * DIFF_HEADS, 3 * LANES), F32),
        name="bias_decode",
    )(rel_bias)


def _diff_proj_body(x_ref, g_ref, w_ref, o_ref, h_ref):
    @pl.when(pl.program_id(1) == 0)
    def _():
        h_ref[...] = _rms(x_ref[...], g_ref[...]).astype(BF16)

    o_ref[...] = _dot(h_ref[...], w_ref[...])


def _diff_proj(x, g, w_in):
    n = x.shape[0]
    tm = min(512, n)
    tn = 512
    nout = w_in.shape[1]
    return pl.pallas_call(
        _diff_proj_body,
        grid=(n // tm, nout // tn),
        in_specs=[
            pl.BlockSpec((tm, D_MODEL), lambda i, j: (i, 0)),
            pl.BlockSpec((1, D_MODEL), lambda i, j: (0, 0)),
            pl.BlockSpec((D_MODEL, tn), lambda i, j: (0, j)),
        ],
        out_specs=pl.BlockSpec((tm, tn), lambda i, j: (i, j)),
        out_shape=jax.ShapeDtypeStruct((n, nout), F32),
        scratch_shapes=[pltpu.VMEM((tm, D_MODEL), BF16)],
        compiler_params=_params(("parallel", "arbitrary")),
        name="diff_proj",
    )(x, g, w_in)


def _lam_full(lam_ref, lam_init):
    lf = lam_ref[...]
    a = jnp.sum(lf[0:1] * lf[1:2], axis=-1, keepdims=True)
    b = jnp.sum(lf[2:3] * lf[3:4], axis=-1, keepdims=True)
    return jnp.exp(a) - jnp.exp(b) + lam_init


def _diff_combine(o0, o1, lam, subln, lam_init):
    a = o0 - lam * o1
    return _rms(a, subln) * (1.0 - lam_init)


def _diff_attn_body(qi_ref, ki_ref, q_ref, k_ref, v_ref, bias_ref, lam_ref, sub_ref, o_ref,
                    q0_ref, q1_ref, m_ref, l_ref, acc_ref, *, lam_init):
    t = pl.program_id(2)
    qi = qi_ref[t]
    ki = ki_ref[t]

    @pl.when(ki == 0)
    def _():
        q = (q_ref[...] * DIFF_SCALE).astype(BF16)
        lane = lax.broadcasted_iota(jnp.int32, q.shape, 1)
        zero = jnp.zeros(q.shape, BF16)
        q0_ref[...] = jnp.where(lane < DIFF_HEAD, q, zero)
        q1_ref[...] = jnp.where(lane < DIFF_HEAD, zero, q)
        m_ref[...] = jnp.full(m_ref.shape, NEG_INF, F32)
        l_ref[...] = jnp.zeros(l_ref.shape, F32)
        acc_ref[...] = jnp.zeros(acc_ref.shape, F32)

    def update(masked):
        k = k_ref[...].astype(BF16)
        v_aug = _with_ones(v_ref[...].astype(BF16))
        for mp, qm_ref in enumerate((q0_ref, q1_ref)):
            _attend_tile(lambda rows, nk: _dot_nt(qm_ref[rows, :], k[:nk]) + bias_ref[rows, :nk],
                         v_aug, m_ref.at[mp], l_ref.at[mp], acc_ref.at[mp], masked, ATTN_ROW_CHUNK)

    @pl.when(ki < qi)
    def _():
        update(False)

    @pl.when(ki == qi)
    def _():
        update(True)
        lam = _lam_full(lam_ref, lam_init)
        o0 = acc_ref[0] / l_ref[0]
        o1 = acc_ref[1] / l_ref[1]
        o_ref[...] = _diff_combine(o0, o1, lam, sub_ref[...], lam_init).astype(BF16)


def _diff_attn(qkv, bias_tiles, lam, subln, lam_init):
    t = ATTN_TILE
    nblk = SEQ // t
    qi, ki = _tri_schedule(nblk)
    kcol0 = DIFF_Q // LANES
    vcol0 = (DIFF_Q + DIFF_K) // LANES
    grid_spec = pltpu.PrefetchScalarGridSpec(
        num_scalar_prefetch=2,
        grid=(BATCH, DIFF_HEADS, qi.shape[0]),
        in_specs=[
            pl.BlockSpec((t, LANES), lambda b, h, s, qi, ki: (b * nblk + qi[s], h)),
            pl.BlockSpec((t, LANES),
                         lambda b, h, s, qi, ki: (b * nblk + ki[s], kcol0 + h // DIFF_GROUP)),
            pl.BlockSpec((t, LANES),
                         lambda b, h, s, qi, ki: (b * nblk + ki[s], vcol0 + h // DIFF_GROUP)),
            pl.BlockSpec((None, None, t, t),
                         lambda b, h, s, qi, ki: (h, jnp.minimum(qi[s] - ki[s], 2), 0, 0)),
            pl.BlockSpec((4, DIFF_HEAD), lambda b, h, s, qi, ki: (0, 0)),
            pl.BlockSpec((1, DIFF_VHEAD), lambda b, h, s, qi, ki: (0, 0)),
        ],
        out_specs=pl.BlockSpec((t, LANES), lambda b, h, s, qi, ki: (b * nblk + qi[s], h)),
        scratch_shapes=[pltpu.VMEM((t, LANES), BF16), pltpu.VMEM((t, LANES), BF16),
                        pltpu.VMEM((2, t, LANES), F32), pltpu.VMEM((2, t, LANES), F32),
                        pltpu.VMEM((2, t, DIFF_VHEAD), F32)],
    )
    return pl.pallas_call(
        functools.partial(_diff_attn_body, lam_init=lam_init),
        grid_spec=grid_spec,
        out_shape=jax.ShapeDtypeStruct((BATCH * SEQ, DIFF_HEADS * DIFF_VHEAD), BF16),
        compiler_params=_params(("parallel", "parallel", "arbitrary")),
        name="diff_attn",
    )(qi, ki, qkv, qkv, qkv, bias_tiles, lam, subln)


def _diff_side(page_table, row0, n_rows, q32, k_new, v_new, dec_bias, lam, subln, cache_kt,
               cache_v2, lam_init):
    nrow = 2 * DIFF_HEADS
    half = nrow // 2
    rows = slice(row0, row0 + n_rows)
    inputs = (q32[rows], k_new[rows].reshape(n_rows, 1, DIFF_K), v_new[rows].reshape(n_rows, 1, DIFF_V),
              dec_bias, lam, subln, cache_kt, cache_v2)
    in_specs = tuple(_resident(a.shape) for a in inputs[:6]) + (
        pl.BlockSpec(memory_space=pl.ANY), pl.BlockSpec(memory_space=pl.ANY))
    scratch = (pltpu.VMEM((RING_SLOTS, PAGES_PER_STEP, DIFF_K, PAGE_SIZE), F32),
               pltpu.VMEM((RING_SLOTS, PAGES_PER_STEP, 2 * PAGE_SIZE, DIFF_VHEAD), F32),
               pltpu.VMEM((DIFF_K, KEYS_PER_STEP), BF16),
               pltpu.VMEM((KEYS_PER_STEP, DIFF_VHEAD), BF16),
               pltpu.VMEM((KEYS_PER_STEP, DIFF_VHEAD), BF16),
               pltpu.VMEM((nrow, DIFF_K), BF16),
               pltpu.VMEM((nrow, LANES), F32), pltpu.VMEM((nrow, LANES), F32),
               pltpu.VMEM((nrow, DIFF_VHEAD), F32),
               pltpu.SemaphoreType.DMA((RING_SLOTS,)))

    def begin(step, pt_ref, in_refs, scratch_refs):
        _side_begin(step, pt_ref, in_refs[6:8], scratch_refs[:2], scratch_refs[-1], row0)

    def run(step, pt_ref, in_refs, o_ref, scratch_refs):
        q_ref, knew_ref, vnew_ref, bias_ref, lam_ref, sub_ref, kt_hbm, v_hbm = in_refs
        kbuf, vbuf, ktb_ref, v0_ref, v1_ref, qbig_ref, m_ref, l_ref, acc_ref, sem = scratch_refs

        def unit(row, sub, i, slot):
            def init():
                r = lax.broadcasted_iota(jnp.int32, (nrow, DIFF_K), 0)
                c = lax.broadcasted_iota(jnp.int32, (nrow, DIFF_K), 1)
                q = q_ref[row] * DIFF_SCALE
                rep = jnp.concatenate([q, q, q, q], axis=-1)
                qbig_ref[...] = jnp.where((c // DIFF_HEAD) == (r // DIFF_GROUP), rep, 0.0).astype(BF16)
                m_ref[...] = jnp.full(m_ref.shape, NEG_INF, F32)
                l_ref[...] = jnp.zeros(l_ref.shape, F32)
                acc_ref[...] = jnp.zeros(acc_ref.shape, F32)

            _when_first_group(sub, i, init)
            for k in range(PAGES_PER_STEP):
                keys = slice(k * PAGE_SIZE, (k + 1) * PAGE_SIZE)
                ktb_ref[:, keys] = kbuf[slot, k].astype(BF16)
                v0_ref[keys, :] = vbuf[slot, k, pl.ds(0, PAGE_SIZE, stride=2), :].astype(BF16)
                v1_ref[keys, :] = vbuf[slot, k, pl.ds(1, PAGE_SIZE, stride=2), :].astype(BF16)
            qbig = qbig_ref[...]
            far = bias_ref[:, LANES:2 * LANES]
            s = _dot(qbig, ktb_ref[...]) + _lane_tile(far, KEYS_PER_STEP)
            if i == UNITS_PER_STEP - 1:
                near_delta = bias_ref[:, :LANES] - far
                tail = (s[:, KEYS_PER_STEP - PAGE_SIZE:]
                        + jnp.where(sub == STEPS_PER_ROW - 1, near_delta, 0.0))
                s = jnp.concatenate([s[:, :KEYS_PER_STEP - PAGE_SIZE], tail], axis=1)
            m, corr, p, l = _softmax_step(s, m_ref[...], l_ref[...])
            pb = p.astype(BF16)
            pv = jnp.concatenate([_dot(pb[:half], v0_ref[...]), _dot(pb[half:], v1_ref[...])], axis=0)
            acc = acc_ref[...] * corr + pv
            m_ref[...] = m
            l_ref[...] = l
            acc_ref[...] = acc

            def finish():
                kn = knew_ref[row].astype(BF16).astype(F32)
                vn = vnew_ref[row].astype(BF16).astype(F32)
                s_new = (jnp.sum(qbig.astype(F32) * kn, axis=-1, keepdims=True)
                         + bias_ref[:, 2 * LANES:])
                m_new = jnp.maximum(m, s_new)
                c2 = jnp.exp(m - m_new)
                p_new = jnp.exp(s_new - m_new)
                l2 = l * c2 + p_new
                rv = lax.broadcasted_iota(jnp.int32, (nrow, DIFF_VHEAD), 0)
                v_row = jnp.where(rv < half, vn[:, :DIFF_VHEAD], vn[:, DIFF_VHEAD:])
                o = (acc * c2 + p_new.astype(BF16).astype(F32) * v_row) / l2
                g8 = DIFF_GROUP
                o0 = jnp.concatenate([o[0:g8], o[2 * g8:3 * g8]], axis=0)
                o1 = jnp.concatenate([o[g8:2 * g8], o[3 * g8:4 * g8]], axis=0)
                lam_full = _lam_full(lam_ref, lam_init)
                o_ref[row] = _diff_combine(o0, o1, lam_full, sub_ref[...], lam_init).astype(BF16)

            _when_last_group(sub, i, finish)

        _side_units(step, pt_ref, (kt_hbm, v_hbm), (kbuf, vbuf), sem, row0, n_rows, unit)

    return _Side(page_table.reshape(-1), inputs, in_specs,
                 jax.ShapeDtypeStruct((n_rows, DIFF_HEADS, DIFF_VHEAD), BF16), scratch, begin, run)


def _rot_half_cols(w):
    half = w.shape[-1] // 2
    return jnp.concatenate([-w[..., half:], w[..., :half]], axis=-1)


def _pad_cols(w, width):
    return jnp.pad(w, [(0, 0)] * (w.ndim - 1) + [(0, width - w.shape[-1])])


def _prep_mla(w_in, q_norm, kv_norm, w_q_up, w_kv_up, w_out):
    w_kr = w_in[:, Q_LORA + KV_LORA:]
    w_in2 = jnp.concatenate([w_in[:, :Q_LORA + KV_LORA], _pad_cols(w_kr, LANES),
                             _pad_cols(_rot_half_cols(w_kr), LANES)], axis=1)
    wq = w_q_up.reshape(Q_LORA, MLA_HEADS, QK_NOPE + QK_ROPE)
    wr = wq[:, :, QK_NOPE:]
    hw = MLA_HEADS * LANES
    wkv = w_kv_up.reshape(KV_LORA, MLA_HEADS, QK_NOPE + V_HEAD)
    return {
        "w_in": w_in2.astype(BF16),
        "q_norm": q_norm.reshape(1, Q_LORA),
        "kv_norm": kv_norm.reshape(1, KV_LORA),
        "w_qn": wq[:, :, :QK_NOPE].reshape(Q_LORA, hw).astype(BF16),
        "w_qa": _pad_cols(wr, LANES).reshape(Q_LORA, hw).astype(BF16),
        "w_qb": _pad_cols(_rot_half_cols(wr), LANES).reshape(Q_LORA, hw).astype(BF16),
        "w_kv": w_kv_up.astype(BF16),
        "w_ukt": jnp.transpose(wkv[:, :, :QK_NOPE], (1, 2, 0)).astype(BF16),
        "w_uv": jnp.transpose(wkv[:, :, QK_NOPE:], (1, 0, 2)).astype(BF16),
        "w_out": w_out.astype(BF16),
    }


def _rope_tables(pos):
    inv = ROPE_THETA ** (-jnp.arange(0, QK_ROPE, 2, dtype=F32) / QK_ROPE)
    ang = pos.astype(F32)[:, None] * inv[None, :]
    reps = LANES // (QK_ROPE // 2)
    return jnp.tile(jnp.cos(ang), (1, reps)), jnp.tile(jnp.sin(ang), (1, reps))


def kernel(x_prompt, x_sample, cache_mla_ckv, cache_mla_krope, cache_diff_k, cache_diff_v, page_table, ffn_norm, w_ffn_up, w_ffn_down, mix_norm, mla_w_in, mla_q_norm, mla_kv_norm, mla_w_q_up, mla_w_kv_up, mla_w_out, diff_w_in, diff_lambda, diff_subln, diff_w_out, rel_bias, final_norm):
    xp = x_prompt.reshape(BATCH * SEQ, D_MODEL)
    xs = x_sample.reshape(DEC_BATCH, D_MODEL)
    w_up = w_ffn_up.astype(BF16)
    w_down = w_ffn_down.astype(BF16)
    gf = final_norm.reshape(1, D_MODEL)
    cos_p, sin_p = _rope_tables(jnp.arange(SEQ, dtype=jnp.int32))
    cos_s, sin_s = _rope_tables(jnp.full((1,), PAST_LEN, jnp.int32))
    n_phys = cache_mla_ckv.shape[1]

    def ffn(x, i, k, final=False, side=None):
        return _ffn(x, ffn_norm[i, k].reshape(1, D_MODEL), w_up, w_down, gf, i, k, final=final,
                    side=side)

    cache_krt = jnp.transpose(cache_mla_krope, (0, 1, 3, 2))
    cache_kt = jnp.transpose(cache_diff_k, (0, 1, 3, 4, 5, 2)).reshape(1, n_phys, DIFF_K, PAGE_SIZE)
    cache_v2 = cache_diff_v.reshape(1, n_phys, 2 * PAGE_SIZE, DIFF_VHEAD)
    half = DEC_BATCH // 2

    xs = ffn(xs, 0, 0)
    g0 = mix_norm[0].reshape(1, D_MODEL)
    wm = _prep_mla(mla_w_in[0], mla_q_norm[0], mla_kv_norm[0], mla_w_q_up[0], mla_w_kv_up[0],
                   mla_w_out[0])
    ckv_s, kr_s, qn_s, qr_s, _, _ = _mla_proj(xs, g0, wm, cos_s, sin_s)
    q_dec = jnp.transpose(_mla_qlat(qn_s, qr_s, wm["w_ukt"]), (1, 0, 2))

    def mla_side(row0):
        return _mla_side(page_table, row0, half, q_dec, ckv_s, kr_s, cache_mla_ckv, cache_krt)

    xp, o_lat_a = ffn(xp, 0, 0, side=mla_side(0))
    ckv_p, kr_p, qn_p, qr_p, kv_p, krb_p = _mla_proj(xp, g0, wm, cos_p, sin_p)
    o_p = _mla_attn(qn_p, qr_p, kv_p, krb_p)
    xp = _outproj(xp, o_p, wm["w_out"])
    xp, o_lat_b = ffn(xp, 0, 1, side=mla_side(half))

    o_lat = jnp.concatenate([o_lat_a, o_lat_b], axis=0)
    xs = _mla_finish(xs, jnp.transpose(o_lat, (1, 0, 2)), wm["w_uv"], wm["w_out"])
    xs = ffn(xs, 0, 1)

    xs = ffn(xs, 1, 0)
    lam_init = 0.8 - 0.6 * math.exp(-0.3 * 1)
    g1 = mix_norm[1].reshape(1, D_MODEL)
    w_din = diff_w_in[0].astype(BF16)
    w_dout = diff_w_out[0].astype(BF16)
    lam = diff_lambda[0]
    subln = diff_subln[0].reshape(1, DIFF_VHEAD)
    bias_tiles = _bias_tiles(rel_bias)
    dec_bias = _bias_decode(rel_bias)

    qkv_s = _diff_proj(xs, g1, w_din)
    q32 = qkv_s[:, :DIFF_Q].reshape(DEC_BATCH, DIFF_KV_HEADS, DIFF_GROUP, 2, DIFF_HEAD)
    q32 = jnp.transpose(q32, (0, 1, 3, 2, 4)).reshape(DEC_BATCH, 2 * DIFF_HEADS, DIFF_HEAD)
    k_s = qkv_s[:, DIFF_Q:DIFF_Q + DIFF_K]
    v_s = qkv_s[:, DIFF_Q + DIFF_K:]

    def diff_side(row0):
        return _diff_side(page_table, row0, half, q32, k_s, v_s, dec_bias, lam, subln, cache_kt,
                          cache_v2, lam_init)

    xp, a_s_a = ffn(xp, 1, 0, side=diff_side(0))
    qkv_p = _diff_proj(xp, g1, w_din)
    a_p = _diff_attn(qkv_p, bias_tiles, lam, subln, lam_init)
    xp = _outproj(xp, a_p, w_dout)
    xp, a_s_b = ffn(xp, 1, 1, final=True, side=diff_side(half))

    a_s = jnp.concatenate([a_s_a, a_s_b], axis=0)
    xs = _outproj(xs, a_s.reshape(DEC_BATCH, DIFF_HEADS * DIFF_VHEAD), w_dout)
    xs = ffn(xs, 1, 1, final=True)

    k_p = qkv_p[:, DIFF_Q:DIFF_Q + DIFF_K]
    v_p = qkv_p[:, DIFF_Q + DIFF_K:]
    return (
        xp.reshape(BATCH, SEQ, D_MODEL),
        xs.reshape(DEC_BATCH, 1, D_MODEL),
        ckv_p.reshape(1, BATCH, SEQ, KV_LORA),
        kr_p.reshape(1, BATCH, SEQ, QK_ROPE),
        k_p.reshape(1, BATCH, SEQ, DIFF_KV_HEADS, 2, DIFF_HEAD),
        v_p.reshape(1, BATCH, SEQ, DIFF_KV_HEADS, DIFF_VHEAD),
        ckv_s.reshape(1, DEC_BATCH, 1, KV_LORA),
        kr_s.reshape(1, DEC_BATCH, 1, QK_ROPE),
        k_s.reshape(1, DEC_BATCH, 1, DIFF_KV_HEADS, 2, DIFF_HEAD),
        v_s.reshape(1, DEC_BATCH, 1, DIFF_KV_HEADS, DIFF_VHEAD),
    )
```

```python
import functools
import math
from typing import Any, Callable, NamedTuple

import jax
import jax.numpy as jnp
import numpy as np
from jax import lax
from jax.experimental import pallas as pl
from jax.experimental.pallas import tpu as pltpu

D_MODEL = 2048
BATCH = 4
SEQ = 2048
DEPTH = 2
DEC_BATCH = 128
PAST_LEN = 16384
PAGE_SIZE = 128
N_PAGES = PAST_LEN // PAGE_SIZE
D_FF = 5632
EPS = 1e-6
MLA_HEADS = 16
Q_LORA = 512
KV_LORA = 512
QK_NOPE = 128
QK_ROPE = 64
V_HEAD = 128
MLA_SCALE = (QK_NOPE + QK_ROPE) ** -0.5
ROPE_THETA = 10000.0
DIFF_HEADS = 16
DIFF_KV_HEADS = 2
DIFF_GROUP = DIFF_HEADS // DIFF_KV_HEADS
DIFF_HEAD = 64
DIFF_VHEAD = 128
DIFF_Q = DIFF_HEADS * 2 * DIFF_HEAD
DIFF_K = DIFF_KV_HEADS * 2 * DIFF_HEAD
DIFF_V = DIFF_KV_HEADS * DIFF_VHEAD
DIFF_SCALE = DIFF_HEAD ** -0.5
NUM_BUCKETS = 32
MAX_DISTANCE = 128
NEG_INF = -1e30

LANES = 128
V7X_VMEM_LIMIT_BYTES = 60000 * 1024

FFN_ROW_TILE = 512
FFN_FF_TILE = 512
PROJ_ROW_TILE = 256
ATTN_TILE = 512
ATTN_ROW_CHUNK = 256
PAGES_PER_STEP = 16
KEYS_PER_STEP = PAGES_PER_STEP * PAGE_SIZE
PAGE_GROUPS = N_PAGES // PAGES_PER_STEP
UNITS_PER_STEP = 4
STEPS_PER_ROW = PAGE_GROUPS // UNITS_PER_STEP
RING_SLOTS = 4
PREFETCH = RING_SLOTS - 1
assert UNITS_PER_STEP % RING_SLOTS == 0
assert math.frexp(DIFF_SCALE)[0] == 0.5
BF16 = jnp.bfloat16
F32 = jnp.float32


def _params(sem):
    return pltpu.CompilerParams(dimension_semantics=sem, vmem_limit_bytes=V7X_VMEM_LIMIT_BYTES)


def _rms(x, g):
    return x * lax.rsqrt(jnp.mean(x * x, axis=-1, keepdims=True) + EPS) * g


def _dot(a, b):
    return jnp.dot(a, b, preferred_element_type=F32)


def _dot_nt(a, b):
    return lax.dot_general(a, b, (((1,), (1,)), ((), ())), preferred_element_type=F32)


def _resident(shape):
    nd = len(shape)
    return pl.BlockSpec(shape, lambda *_: (0,) * nd, pipeline_mode=pl.Buffered(1))


class _Side(NamedTuple):
    page_table: Any
    inputs: tuple
    in_specs: tuple
    out_shape: Any
    scratch: tuple
    begin: Callable
    run: Callable


def _ffn_body(*refs, final, side):
    n_side_in = len(side.inputs) if side else 0
    pt_ref = refs[0] if side else None
    refs = refs[1:] if side else refs
    x_ref, g_ref, wg_ref, wu_ref, wd_ref, gf_ref = refs[:6]
    side_in = refs[6:6 + n_side_in]
    o_ref = refs[6 + n_side_in]
    rest = refs[7 + n_side_in:]
    side_out, rest = (rest[0], rest[1:]) if side else (None, rest)
    h_ref, side_scratch = rest[0], rest[1:]
    j = pl.program_id(1)
    step = pl.program_id(0) * pl.num_programs(1) + j

    if side:
        side.begin(step, pt_ref, side_in, side_scratch)

    @pl.when(j == 0)
    def _():
        x = x_ref[...]
        h_ref[...] = _rms(x, g_ref[...]).astype(BF16)
        o_ref[...] = x

    h = h_ref[...]
    gate = _dot(h, wg_ref[...])
    up = _dot(h, wu_ref[...])
    a = (gate / (1.0 + jnp.exp(-gate)) * up).astype(BF16)
    o_ref[...] += 0.5 * _dot(a, wd_ref[...])

    if final:
        @pl.when(j == pl.num_programs(1) - 1)
        def _():
            o_ref[...] = _rms(o_ref[...], gf_ref[...])

    if side:
        side.run(step, pt_ref, side_in, side_out, side_scratch)


def _ffn(x, g, w_up, w_down, gf, layer, pos, *, final, side=None):
    n = x.shape[0]
    tm = min(FFN_ROW_TILE, n)
    tf = FFN_FF_TILE
    nf = D_FF // tf
    in_specs = [
        pl.BlockSpec((tm, D_MODEL), lambda i, j, *_: (i, 0)),
        pl.BlockSpec((1, D_MODEL), lambda i, j, *_: (0, 0)),
        pl.BlockSpec((None, None, D_MODEL, tf), lambda i, j, *_: (layer, pos, 0, j)),
        pl.BlockSpec((None, None, D_MODEL, tf), lambda i, j, *_: (layer, pos, 0, j + nf)),
        pl.BlockSpec((None, None, tf, D_MODEL), lambda i, j, *_: (layer, pos, j, 0)),
        pl.BlockSpec((1, D_MODEL), lambda i, j, *_: (0, 0)),
    ]
    out_specs = pl.BlockSpec((tm, D_MODEL), lambda i, j, *_: (i, 0))
    out_shape = jax.ShapeDtypeStruct((n, D_MODEL), F32)
    scratch = [pltpu.VMEM((tm, D_MODEL), BF16)]
    operands = (x, g, w_up, w_up, w_down, gf)
    if side:
        assert (n // tm) * nf >= side.out_shape.shape[0] * PAGE_GROUPS // UNITS_PER_STEP
        in_specs += list(side.in_specs)
        nd_side = len(side.out_shape.shape)
        out_specs = [out_specs, pl.BlockSpec(side.out_shape.shape, lambda *_: (0,) * nd_side)]
        out_shape = [out_shape, side.out_shape]
        scratch += list(side.scratch)
        operands = (side.page_table,) + operands + tuple(side.inputs)
    grid_spec = pltpu.PrefetchScalarGridSpec(
        num_scalar_prefetch=1 if side else 0, grid=(n // tm, nf),
        in_specs=in_specs, out_specs=out_specs, scratch_shapes=scratch)
    sem = ("arbitrary", "arbitrary") if side else ("parallel", "arbitrary")
    return pl.pallas_call(
        functools.partial(_ffn_body, final=final, side=side),
        grid_spec=grid_spec,
        out_shape=out_shape,
        compiler_params=_params(sem),
        name="ffn_decode" if side else "ffn",
    )(*operands)


def _outproj_body(x_ref, a_ref, w_ref, o_ref):
    o_ref[...] = x_ref[...] + _dot(a_ref[...], w_ref[...])


def _outproj(x, a, w):
    n = x.shape[0]
    tm = min(512, n)
    return pl.pallas_call(
        _outproj_body,
        grid=(n // tm,),
        in_specs=[
            pl.BlockSpec((tm, D_MODEL), lambda i: (i, 0)),
            pl.BlockSpec((tm, a.shape[1]), lambda i: (i, 0)),
            _resident(w.shape),
        ],
        out_specs=pl.BlockSpec((tm, D_MODEL), lambda i: (i, 0)),
        out_shape=jax.ShapeDtypeStruct((n, D_MODEL), F32),
        compiler_params=_params(("parallel",)),
        name="outproj",
    )(x, a, w)


def _mla_proj_body(x_ref, g_ref, win_ref, qn_ref, kvn_ref, cos_ref, sin_ref,
                   wqn_ref, wqa_ref, wqb_ref, wkv_ref,
                   ckv_ref, kr_ref, qnope_ref, qrope_ref, kv_ref, krb_ref):
    h = _rms(x_ref[...], g_ref[...]).astype(BF16)
    proj = _dot(h, win_ref[...])
    q_c = _rms(proj[:, :Q_LORA], qn_ref[...]).astype(BF16)
    ckv = _rms(proj[:, Q_LORA:Q_LORA + KV_LORA], kvn_ref[...])
    cos = cos_ref[...]
    sin = sin_ref[...]
    kr = proj[:, 1024:1152] * cos + proj[:, 1152:1280] * sin
    ckv_ref[...] = ckv
    kr_ref[...] = kr[:, :QK_ROPE]
    ckv_b = ckv.astype(BF16)
    krb_ref[...] = kr.astype(BF16)
    qnope_ref[...] = _dot(q_c, wqn_ref[...]).astype(BF16)
    qa = _dot(q_c, wqa_ref[...])
    qb = _dot(q_c, wqb_ref[...])
    for hd in range(MLA_HEADS):
        sl = slice(hd * LANES, (hd + 1) * LANES)
        qrope_ref[:, sl] = (qa[:, sl] * cos + qb[:, sl] * sin).astype(BF16)
    kv_ref[...] = _dot(ckv_b, wkv_ref[...]).astype(BF16)


def _mla_proj(x, g, w, cos_t, sin_t):
    n = x.shape[0]
    tm = min(PROJ_ROW_TILE, n)
    tt = cos_t.shape[0]
    tb = tm if tt > 1 else 1
    nt = tt // tb
    row = lambda i: (i, 0)
    hw = MLA_HEADS * LANES
    outs = [
        jax.ShapeDtypeStruct((n, KV_LORA), F32),
        jax.ShapeDtypeStruct((n, QK_ROPE), F32),
        jax.ShapeDtypeStruct((n, hw), BF16),
        jax.ShapeDtypeStruct((n, hw), BF16),
        jax.ShapeDtypeStruct((n, 2 * hw), BF16),
        jax.ShapeDtypeStruct((n, LANES), BF16),
    ]
    return pl.pallas_call(
        _mla_proj_body,
        grid=(n // tm,),
        in_specs=[
            pl.BlockSpec((tm, D_MODEL), row),
            _resident((1, D_MODEL)),
            _resident(w["w_in"].shape),
            _resident((1, Q_LORA)),
            _resident((1, KV_LORA)),
            pl.BlockSpec((tb, LANES), lambda i: (i % nt, 0)),
            pl.BlockSpec((tb, LANES), lambda i: (i % nt, 0)),
            _resident(w["w_qn"].shape),
            _resident(w["w_qa"].shape),
            _resident(w["w_qb"].shape),
            _resident(w["w_kv"].shape),
        ],
        out_specs=[pl.BlockSpec((tm, s.shape[1]), row) for s in outs],
        out_shape=outs,
        compiler_params=_params(("parallel",)),
        name="mla_proj",
    )(x, g, w["w_in"], w["q_norm"], w["kv_norm"], cos_t, sin_t,
      w["w_qn"], w["w_qa"], w["w_qb"], w["w_kv"])


def _tri_schedule(nblk):
    qi, ki = [], []
    for q in range(nblk):
        for k in range(q + 1):
            qi.append(q)
            ki.append(k)
    return jnp.asarray(qi, jnp.int32), jnp.asarray(ki, jnp.int32)


def _lane_tile(x, width):
    return jnp.concatenate([x] * (width // LANES), axis=1)


def _with_ones(v):
    return jnp.concatenate([v, jnp.ones(v.shape, BF16)], axis=1)


def _online(s, v_aug, m_ref, l_ref, acc_ref):
    m_old = m_ref[...]
    m_new = jnp.maximum(m_old, jnp.max(s, axis=-1, keepdims=True))
    corr = jnp.exp(m_old - m_new)
    p = jnp.exp(s - _lane_tile(m_new, s.shape[1]))
    pv = _dot(p.astype(BF16), v_aug)
    l_ref[...] = l_ref[...] * corr + pv[:, LANES:]
    acc_ref[...] = acc_ref[...] * corr + pv[:, :LANES]
    m_ref[...] = m_new


def _attend_tile(score_fn, v_aug, m_ref, l_ref, acc_ref, masked, chunk):
    t = v_aug.shape[0]
    for r0 in range(0, t, chunk):
        nk = r0 + chunk if masked else t
        rows = slice(r0, r0 + chunk)
        s = score_fn(rows, nk)
        if masked:
            r = lax.broadcasted_iota(jnp.int32, s.shape, 0) + r0
            c = lax.broadcasted_iota(jnp.int32, s.shape, 1)
            s = jnp.where(c <= r, s, NEG_INF)
        _online(s, v_aug[:nk], m_ref.at[rows], l_ref.at[rows], acc_ref.at[rows])


def _mla_attn_body(qi_ref, ki_ref, qn_ref, qr_ref, kn_ref, kr_ref, v_ref, o_ref,
                   q_ref, m_ref, l_ref, acc_ref):
    t = pl.program_id(2)
    qi = qi_ref[t]
    ki = ki_ref[t]

    @pl.when(ki == 0)
    def _():
        q_ref[...] = jnp.concatenate([qn_ref[...], qr_ref[...]], axis=1)
        m_ref[...] = jnp.full(m_ref.shape, NEG_INF, F32)
        l_ref[...] = jnp.zeros(l_ref.shape, F32)
        acc_ref[...] = jnp.zeros(acc_ref.shape, F32)

    k = jnp.concatenate([kn_ref[...], kr_ref[...]], axis=1)
    s = _dot_nt(q_ref[...], k) * MLA_SCALE
    v_aug = _with_ones(v_ref[...])

    def update(masked):
        _attend_tile(lambda rows, nk: s[rows, :nk], v_aug, m_ref, l_ref, acc_ref, masked, ATTN_TILE)

    @pl.when(ki < qi)
    def _():
        update(False)

    @pl.when(ki == qi)
    def _():
        update(True)
        o_ref[...] = (acc_ref[...] / l_ref[...]).astype(BF16)


def _mla_attn(q_nope, q_rope, kv, kr_b):
    t = ATTN_TILE
    nblk = SEQ // t
    qi, ki = _tri_schedule(nblk)
    qmap = lambda b, h, s, qi, ki: (b * nblk + qi[s], h)
    grid_spec = pltpu.PrefetchScalarGridSpec(
        num_scalar_prefetch=2,
        grid=(BATCH, MLA_HEADS, qi.shape[0]),
        in_specs=[
            pl.BlockSpec((t, LANES), qmap),
            pl.BlockSpec((t, LANES), qmap),
            pl.BlockSpec((t, LANES), lambda b, h, s, qi, ki: (b * nblk + ki[s], 2 * h)),
            pl.BlockSpec((t, LANES), lambda b, h, s, qi, ki: (b * nblk + ki[s], 0)),
            pl.BlockSpec((t, LANES), lambda b, h, s, qi, ki: (b * nblk + ki[s], 2 * h + 1)),
        ],
        out_specs=pl.BlockSpec((t, LANES), qmap),
        scratch_shapes=[pltpu.VMEM((t, 2 * LANES), BF16),
                        pltpu.VMEM((t, LANES), F32), pltpu.VMEM((t, LANES), F32),
                        pltpu.VMEM((t, V_HEAD), F32)],
    )
    return pl.pallas_call(
        _mla_attn_body,
        grid_spec=grid_spec,
        out_shape=jax.ShapeDtypeStruct((BATCH * SEQ, MLA_HEADS * V_HEAD), BF16),
        compiler_params=_params(("parallel", "parallel", "arbitrary")),
        name="mla_attn",
    )(qi, ki, q_nope, q_rope, kv, kr_b, kv)


def _mla_qlat_body(qn_ref, qr_ref, wukt_ref, o_ref):
    for hd in range(MLA_HEADS):
        ql = _dot(qn_ref[:, hd * LANES:(hd + 1) * LANES], wukt_ref[hd])
        o_ref[hd, :, :KV_LORA] = ql.astype(BF16)
        o_ref[hd, :, KV_LORA:] = qr_ref[:, hd * LANES:(hd + 1) * LANES]


def _mla_qlat(q_nope, q_rope, w_ukt):
    return pl.pallas_call(
        _mla_qlat_body,
        out_shape=jax.ShapeDtypeStruct((MLA_HEADS, DEC_BATCH, KV_LORA + LANES), BF16),
        compiler_params=_params(None),
        name="mla_qlat",
    )(q_nope, q_rope, w_ukt)


def _group_copies(page_of, hbm_refs, bufs, sem, slot):
    copies = []
    for k in range(PAGES_PER_STEP):
        page = page_of(k)
        for hbm, buf in zip(hbm_refs, bufs):
            copies.append(pltpu.make_async_copy(hbm.at[0, page], buf.at[slot, k], sem.at[slot]))
    return copies


def _start_group(pt_ref, base, hbm_refs, bufs, sem, slot):
    copies = _group_copies(lambda k: pt_ref[base + k], hbm_refs, bufs, sem, slot)
    for i, c in enumerate(copies):
        c.start(priority=(i // len(hbm_refs)) % 2)


def _wait_group(hbm_refs, bufs, sem, slot):
    for c in _group_copies(lambda k: 0, hbm_refs, bufs, sem, slot):
        c.wait()


def _side_begin(step, pt_ref, hbm_refs, bufs, sem, row0):
    @pl.when(step == 0)
    def _():
        for d in range(PREFETCH):
            _start_group(pt_ref, row0 * N_PAGES + d * PAGES_PER_STEP, hbm_refs, bufs, sem, d)


def _side_units(step, pt_ref, hbm_refs, bufs, sem, row0, n_rows, unit_fn):
    n_steps = n_rows * STEPS_PER_ROW
    first_base = row0 * N_PAGES
    last_base = (row0 + n_rows) * N_PAGES - PAGES_PER_STEP

    @pl.when(step < n_steps)
    def _():
        for i in range(UNITS_PER_STEP):
            slot = i % RING_SLOTS
            nxt = first_base + (step * UNITS_PER_STEP + i + PREFETCH) * PAGES_PER_STEP
            _start_group(pt_ref, jnp.minimum(nxt, last_base), hbm_refs, bufs, sem,
                         (i + PREFETCH) % RING_SLOTS)
            _wait_group(hbm_refs, bufs, sem, slot)
            unit_fn(step // STEPS_PER_ROW, step % STEPS_PER_ROW, i, slot)

    @pl.when(step == n_steps - 1)
    def _():
        for d in range(PREFETCH):
            _wait_group(hbm_refs, bufs, sem, d)


def _when_first_group(sub, i, fn):
    if i == 0:
        pl.when(sub == 0)(fn)


def _when_last_group(sub, i, fn):
    if i == UNITS_PER_STEP - 1:
        pl.when(sub == STEPS_PER_ROW - 1)(fn)


def _softmax_step(s, m, l):
    m_new = jnp.maximum(m, jnp.max(s, axis=-1, keepdims=True))
    corr = jnp.exp(m - m_new)
    p = jnp.exp(s - _lane_tile(m_new, s.shape[1]))
    return m_new, corr, p, l * corr + jnp.sum(p, axis=-1, keepdims=True)


def _mla_side(page_table, row0, n_rows, q_dec, ckv_new, kr_new, cache_ckv, cache_krt):
    rows = slice(row0, row0 + n_rows)
    inputs = (q_dec[rows], ckv_new[rows].reshape(n_rows, 1, KV_LORA),
              kr_new[rows].reshape(n_rows, 1, QK_ROPE), cache_ckv, cache_krt)
    in_specs = (_resident(inputs[0].shape), _resident(inputs[1].shape), _resident(inputs[2].shape),
                pl.BlockSpec(memory_space=pl.ANY), pl.BlockSpec(memory_space=pl.ANY))
    scratch = (pltpu.VMEM((RING_SLOTS, PAGES_PER_STEP, PAGE_SIZE, KV_LORA), F32),
               pltpu.VMEM((RING_SLOTS, PAGES_PER_STEP, QK_ROPE, PAGE_SIZE), F32),
               pltpu.VMEM((KEYS_PER_STEP, KV_LORA), BF16),
               pltpu.VMEM((QK_ROPE, KEYS_PER_STEP), BF16),
               pltpu.VMEM((MLA_HEADS, LANES), F32), pltpu.VMEM((MLA_HEADS, LANES), F32),
               pltpu.VMEM((MLA_HEADS, KV_LORA), F32),
               pltpu.SemaphoreType.DMA((RING_SLOTS,)))

    def begin(step, pt_ref, in_refs, scratch_refs):
        cbuf, rbuf = scratch_refs[:2]
        _side_begin(step, pt_ref, in_refs[3:5], (cbuf, rbuf), scratch_refs[-1], row0)

    def run(step, pt_ref, in_refs, o_ref, scratch_refs):
        q_ref, cnew_ref, krnew_ref, ckv_hbm, krt_hbm = in_refs
        cbuf, rbuf, kc_ref, krt_ref, m_ref, l_ref, acc_ref, sem = scratch_refs

        def unit(row, sub, i, slot):
            def init():
                m_ref[...] = jnp.full(m_ref.shape, NEG_INF, F32)
                l_ref[...] = jnp.zeros(l_ref.shape, F32)
                acc_ref[...] = jnp.zeros(acc_ref.shape, F32)

            _when_first_group(sub, i, init)
            q = q_ref[row]
            q_lat = q[:, :KV_LORA]
            q_rope = q[:, KV_LORA:KV_LORA + QK_ROPE]
            for k in range(PAGES_PER_STEP):
                keys = slice(k * PAGE_SIZE, (k + 1) * PAGE_SIZE)
                kc_ref[keys, :] = cbuf[slot, k].astype(BF16)
                krt_ref[:, keys] = rbuf[slot, k].astype(BF16)
            kc = kc_ref[...]
            s = (_dot_nt(q_lat, kc) + _dot(q_rope, krt_ref[...])) * MLA_SCALE
            m, corr, p, l = _softmax_step(s, m_ref[...], l_ref[...])
            acc = acc_ref[...] * _lane_tile(corr, KV_LORA) + _dot(p.astype(BF16), kc)
            m_ref[...] = m
            l_ref[...] = l
            acc_ref[...] = acc

            def finish():
                cnew = cnew_ref[row].astype(BF16).astype(F32)
                krnew = krnew_ref[row].astype(BF16).astype(F32)
                s_new = (jnp.sum(q_lat.astype(F32) * cnew, axis=-1, keepdims=True)
                         + jnp.sum(q_rope.astype(F32) * krnew, axis=-1, keepdims=True)) * MLA_SCALE
                m_new = jnp.maximum(m, s_new)
                c2 = jnp.exp(m - m_new)
                p_new = jnp.exp(s_new - m_new)
                l2 = l * c2 + p_new
                acc2 = (acc * _lane_tile(c2, KV_LORA)
                        + _lane_tile(p_new.astype(BF16).astype(F32), KV_LORA) * cnew)
                o_ref[row] = acc2 / _lane_tile(l2, KV_LORA)

            _when_last_group(sub, i, finish)

        _side_units(step, pt_ref, (ckv_hbm, krt_hbm), (cbuf, rbuf), sem, row0, n_rows, unit)

    return _Side(page_table.reshape(-1), inputs, in_specs,
                 jax.ShapeDtypeStruct((n_rows, MLA_HEADS, KV_LORA), F32), scratch, begin, run)


def _mla_finish_body(x_ref, olat_ref, wuv_ref, wout_ref, o_ref, o_scr):
    for hd in range(MLA_HEADS):
        o_scr[:, hd * V_HEAD:(hd + 1) * V_HEAD] = _dot(
            olat_ref[hd].astype(BF16), wuv_ref[hd]).astype(BF16)
    o_ref[...] = x_ref[...] + _dot(o_scr[...], wout_ref[...])


def _mla_finish(x, o_lat, w_uv, w_out):
    return pl.pallas_call(
        _mla_finish_body,
        out_shape=jax.ShapeDtypeStruct((DEC_BATCH, D_MODEL), F32),
        scratch_shapes=[pltpu.VMEM((DEC_BATCH, MLA_HEADS * V_HEAD), BF16)],
        compiler_params=_params(None),
        name="mla_finish",
    )(x, o_lat, w_uv, w_out)


def _t5_bucket(d):
    max_exact = NUM_BUCKETS // 2
    nf = jnp.maximum(d, 1).astype(F32)
    large = max_exact + (jnp.log(nf / max_exact) / math.log(MAX_DISTANCE / max_exact)
                         * (NUM_BUCKETS - max_exact)).astype(jnp.int32)
    return jnp.where(d < max_exact, d, jnp.minimum(large, NUM_BUCKETS - 1))


def _bias_lookup(bucket, rb_ref, h):
    out = jnp.zeros(bucket.shape, F32)
    for b in range(NUM_BUCKETS):
        out = jnp.where(bucket == b, rb_ref[b, h], out)
    return out


def _bias_tiles_body(rb_ref, o_ref, bk_ref):
    cls = pl.program_id(0)
    h = pl.program_id(1)

    @pl.when(h == 0)
    def _():
        r = lax.broadcasted_iota(jnp.int32, bk_ref.shape, 0)
        c = lax.broadcasted_iota(jnp.int32, bk_ref.shape, 1)
        d = jnp.maximum(cls * ATTN_TILE + r - c, 0)
        bk_ref[...] = _t5_bucket(d)

    @pl.when(cls < 2)
    def _():
        o_ref[...] = _bias_lookup(bk_ref[...], rb_ref, h)

    @pl.when(cls == 2)
    def _():
        o_ref[...] = jnp.full(o_ref.shape, rb_ref[NUM_BUCKETS - 1, h], F32)


def _bias_tiles(rel_bias):
    assert ATTN_TILE >= MAX_DISTANCE
    t = ATTN_TILE
    return pl.pallas_call(
        _bias_tiles_body,
        grid=(3, DIFF_HEADS),
        in_specs=[pl.BlockSpec(memory_space=pltpu.SMEM)],
        out_specs=pl.BlockSpec((None, None, t, t), lambda c, h: (h, c, 0, 0)),
        out_shape=jax.ShapeDtypeStruct((DIFF_HEADS, 3, t, t), F32),
        scratch_shapes=[pltpu.VMEM((t, t), jnp.int32)],
        compiler_params=_params(("arbitrary", "arbitrary")),
        name="bias_tiles",
    )(rel_bias)


def _bias_decode_body(rb_ref, o_ref):
    c = lax.broadcasted_iota(jnp.int32, (1, 3 * LANES), 1)
    d = jnp.where(c < LANES, PAGE_SIZE - c, jnp.where(c < 2 * LANES, PAST_LEN, 0))
    bucket = _t5_bucket(d)
    for r in range(2 * DIFF_HEADS):
        head = (r // (2 * DIFF_GROUP)) * DIFF_GROUP + r % DIFF_GROUP
        o_ref[r:r + 1, :] = _bias_lookup(bucket, rb_ref, head)


def _bias_decode(rel_bias):
    return pl.pallas_call(
        _bias_decode_body,
        in_specs=[pl.BlockSpec(memory_space=pltpu.SMEM)],
        out_shape=jax.ShapeDtypeStruct((2 * DIFF_HEADS, 3 * LANES), F32),
        name="bias_decode",
    )(rel_bias)


def _diff_proj_body(x_ref, g_ref, w_ref, o_ref, h_ref):
    @pl.when(pl.program_id(1) == 0)
    def _():
        h_ref[...] = _rms(x_ref[...], g_ref[...]).astype(BF16)

    o_ref[...] = _dot(h_ref[...], w_ref[...])


def _diff_proj(x, g, w_in):
    n = x.shape[0]
    tm = min(512, n)
    tn = 512
    nout = w_in.shape[1]
    return pl.pallas_call(
        _diff_proj_body,
        grid=(n // tm, nout // tn),
        in_specs=[
            pl.BlockSpec((tm, D_MODEL), lambda i, j: (i, 0)),
            pl.BlockSpec((1, D_MODEL), lambda i, j: (0, 0)),
            pl.BlockSpec((D_MODEL, tn), lambda i, j: (0, j)),
        ],
        out_specs=pl.BlockSpec((tm, tn), lambda i, j: (i, j)),
        out_shape=jax.ShapeDtypeStruct((n, nout), F32),
        scratch_shapes=[pltpu.VMEM((tm, D_MODEL), BF16)],
        compiler_params=_params(("parallel", "arbitrary")),
        name="diff_proj",
    )(x, g, w_in)


def _lam_full(lam_ref, lam_init):
    lf = lam_ref[...]
    a = jnp.sum(lf[0:1] * lf[1:2], axis=-1, keepdims=True)
    b = jnp.sum(lf[2:3] * lf[3:4], axis=-1, keepdims=True)
    return jnp.exp(a) - jnp.exp(b) + lam_init


def _diff_combine(o0, o1, lam, subln, lam_init):
    a = o0 - lam * o1
    return _rms(a, subln) * (1.0 - lam_init)


def _diff_attn_body(qi_ref, ki_ref, q_ref, k_ref, v_ref, bias_ref, lam_ref, sub_ref, o_ref,
                    q0_ref, q1_ref, m_ref, l_ref, acc_ref, *, lam_init):
    t = pl.program_id(2)
    qi = qi_ref[t]
    ki = ki_ref[t]

    @pl.when(ki == 0)
    def _():
        q = (q_ref[...] * DIFF_SCALE).astype(BF16)
        lane = lax.broadcasted_iota(jnp.int32, q.shape, 1)
        zero = jnp.zeros(q.shape, BF16)
        q0_ref[...] = jnp.where(lane < DIFF_HEAD, q, zero)
        q1_ref[...] = jnp.where(lane < DIFF_HEAD, zero, q)
        m_ref[...] = jnp.full(m_ref.shape, NEG_INF, F32)
        l_ref[...] = jnp.zeros(l_ref.shape, F32)
        acc_ref[...] = jnp.zeros(acc_ref.shape, F32)

    def update(masked):
        k = k_ref[...].astype(BF16)
        v_aug = _with_ones(v_ref[...].astype(BF16))
        for mp, qm_ref in enumerate((q0_ref, q1_ref)):
            _attend_tile(lambda rows, nk: _dot_nt(qm_ref[rows, :], k[:nk]) + bias_ref[rows, :nk],
                         v_aug, m_ref.at[mp], l_ref.at[mp], acc_ref.at[mp], masked, ATTN_ROW_CHUNK)

    @pl.when(ki < qi)
    def _():
        update(False)

    @pl.when(ki == qi)
    def _():
        update(True)
        lam = _lam_full(lam_ref, lam_init)
        o0 = acc_ref[0] / l_ref[0]
        o1 = acc_ref[1] / l_ref[1]
        o_ref[...] = _diff_combine(o0, o1, lam, sub_ref[...], lam_init).astype(BF16)


def _diff_attn(qkv, bias_tiles, lam, subln, lam_init):
    t = ATTN_TILE
    nblk = SEQ // t
    qi, ki = _tri_schedule(nblk)
    kcol0 = DIFF_Q // LANES
    vcol0 = (DIFF_Q + DIFF_K) // LANES
    grid_spec = pltpu.PrefetchScalarGridSpec(
        num_scalar_prefetch=2,
        grid=(BATCH, DIFF_HEADS, qi.shape[0]),
        in_specs=[
            pl.BlockSpec((t, LANES), lambda b, h, s, qi, ki: (b * nblk + qi[s], h)),
            pl.BlockSpec((t, LANES),
                         lambda b, h, s, qi, ki: (b * nblk + ki[s], kcol0 + h // DIFF_GROUP)),
            pl.BlockSpec((t, LANES),
                         lambda b, h, s, qi, ki: (b * nblk + ki[s], vcol0 + h // DIFF_GROUP)),
            pl.BlockSpec((None, None, t, t),
                         lambda b, h, s, qi, ki: (h, jnp.minimum(qi[s] - ki[s], 2), 0, 0)),
            pl.BlockSpec((4, DIFF_HEAD), lambda b, h, s, qi, ki: (0, 0)),
            pl.BlockSpec((1, DIFF_VHEAD), lambda b, h, s, qi, ki: (0, 0)),
        ],
        out_specs=pl.BlockSpec((t, LANES), lambda b, h, s, qi, ki: (b * nblk + qi[s], h)),
        scratch_shapes=[pltpu.VMEM((t, LANES), BF16), pltpu.VMEM((t, LANES), BF16),
                        pltpu.VMEM((2, t, LANES), F32), pltpu.VMEM((2, t, LANES), F32),
                        pltpu.VMEM((2, t, DIFF_VHEAD), F32)],
    )
    return pl.pallas_call(
        functools.partial(_diff_attn_body, lam_init=lam_init),
        grid_spec=grid_spec,
        out_shape=jax.ShapeDtypeStruct((BATCH * SEQ, DIFF_HEADS * DIFF_VHEAD), BF16),
        compiler_params=_params(("parallel", "parallel", "arbitrary")),
        name="diff_attn",
    )(qi, ki, qkv, qkv, qkv, bias_tiles, lam, subln)


def _diff_side(page_table, row0, n_rows, q32, k_new, v_new, dec_bias, lam, subln, cache_kt,
               cache_v2, lam_init):
    nrow = 2 * DIFF_HEADS
    half = nrow // 2
    rows = slice(row0, row0 + n_rows)
    inputs = (q32[rows], k_new[rows].reshape(n_rows, 1, DIFF_K), v_new[rows].reshape(n_rows, 1, DIFF_V),
              dec_bias, lam, subln, cache_kt, cache_v2)
    in_specs = tuple(_resident(a.shape) for a in inputs[:6]) + (
        pl.BlockSpec(memory_space=pl.ANY), pl.BlockSpec(memory_space=pl.ANY))
    scratch = (pltpu.VMEM((RING_SLOTS, PAGES_PER_STEP, DIFF_K, PAGE_SIZE), F32),
               pltpu.VMEM((RING_SLOTS, PAGES_PER_STEP, 2 * PAGE_SIZE, DIFF_VHEAD), F32),
               pltpu.VMEM((DIFF_K, KEYS_PER_STEP), BF16),
               pltpu.VMEM((KEYS_PER_STEP, DIFF_VHEAD), BF16),
               pltpu.VMEM((KEYS_PER_STEP, DIFF_VHEAD), BF16),
               pltpu.VMEM((nrow, DIFF_K), BF16),
               pltpu.VMEM((nrow, LANES), F32), pltpu.VMEM((nrow, LANES), F32),
               pltpu.VMEM((nrow, DIFF_VHEAD), F32),
               pltpu.SemaphoreType.DMA((RING_SLOTS,)))

    def begin(step, pt_ref, in_refs, scratch_refs):
        _side_begin(step, pt_ref, in_refs[6:8], scratch_refs[:2], scratch_refs[-1], row0)

    def run(step, pt_ref, in_refs, o_ref, scratch_refs):
        q_ref, knew_ref, vnew_ref, bias_ref, lam_ref, sub_ref, kt_hbm, v_hbm = in_refs
        kbuf, vbuf, ktb_ref, v0_ref, v1_ref, qbig_ref, m_ref, l_ref, acc_ref, sem = scratch_refs

        def unit(row, sub, i, slot):
            def init():
                r = lax.broadcasted_iota(jnp.int32, (nrow, DIFF_K), 0)
                c = lax.broadcasted_iota(jnp.int32, (nrow, DIFF_K), 1)
                q = q_ref[row] * DIFF_SCALE
                rep = jnp.concatenate([q, q, q, q], axis=-1)
                qbig_ref[...] = jnp.where((c // DIFF_HEAD) == (r // DIFF_GROUP), rep, 0.0).astype(BF16)
                m_ref[...] = jnp.full(m_ref.shape, NEG_INF, F32)
                l_ref[...] = jnp.zeros(l_ref.shape, F32)
                acc_ref[...] = jnp.zeros(acc_ref.shape, F32)

            _when_first_group(sub, i, init)
            for k in range(PAGES_PER_STEP):
                keys = slice(k * PAGE_SIZE, (k + 1) * PAGE_SIZE)
                ktb_ref[:, keys] = kbuf[slot, k].astype(BF16)
                v0_ref[keys, :] = vbuf[slot, k, pl.ds(0, PAGE_SIZE, stride=2), :].astype(BF16)
                v1_ref[keys, :] = vbuf[slot, k, pl.ds(1, PAGE_SIZE, stride=2), :].astype(BF16)
            qbig = qbig_ref[...]
            far = bias_ref[:, LANES:2 * LANES]
            s = _dot(qbig, ktb_ref[...]) + _lane_tile(far, KEYS_PER_STEP)
            if i == UNITS_PER_STEP - 1:
                near_delta = bias_ref[:, :LANES] - far
                tail = (s[:, KEYS_PER_STEP - PAGE_SIZE:]
                        + jnp.where(sub == STEPS_PER_ROW - 1, near_delta, 0.0))
                s = jnp.concatenate([s[:, :KEYS_PER_STEP - PAGE_SIZE], tail], axis=1)
            m, corr, p, l = _softmax_step(s, m_ref[...], l_ref[...])
            pb = p.astype(BF16)
            pv = jnp.concatenate([_dot(pb[:half], v0_ref[...]), _dot(pb[half:], v1_ref[...])], axis=0)
            acc = acc_ref[...] * corr + pv
            m_ref[...] = m
            l_ref[...] = l
            acc_ref[...] = acc

            def finish():
                kn = knew_ref[row].astype(BF16).astype(F32)
                vn = vnew_ref[row].astype(BF16).astype(F32)
                s_new = (jnp.sum(qbig.astype(F32) * kn, axis=-1, keepdims=True)
                         + bias_ref[:, 2 * LANES:])
                m_new = jnp.maximum(m, s_new)
                c2 = jnp.exp(m - m_new)
                p_new = jnp.exp(s_new - m_new)
                l2 = l * c2 + p_new
                rv = lax.broadcasted_iota(jnp.int32, (nrow, DIFF_VHEAD), 0)
                v_row = jnp.where(rv < half, vn[:, :DIFF_VHEAD], vn[:, DIFF_VHEAD:])
                o = (acc * c2 + p_new.astype(BF16).astype(F32) * v_row) / l2
                g8 = DIFF_GROUP
                o0 = jnp.concatenate([o[0:g8], o[2 * g8:3 * g8]], axis=0)
                o1 = jnp.concatenate([o[g8:2 * g8], o[3 * g8:4 * g8]], axis=0)
                lam_full = _lam_full(lam_ref, lam_init)
                o_ref[row] = _diff_combine(o0, o1, lam_full, sub_ref[...], lam_init).astype(BF16)

            _when_last_group(sub, i, finish)

        _side_units(step, pt_ref, (kt_hbm, v_hbm), (kbuf, vbuf), sem, row0, n_rows, unit)

    return _Side(page_table.reshape(-1), inputs, in_specs,
                 jax.ShapeDtypeStruct((n_rows, DIFF_HEADS, DIFF_VHEAD), BF16), scratch, begin, run)


def _rot_half_cols(w):
    half = w.shape[-1] // 2
    return jnp.concatenate([-w[..., half:], w[..., :half]], axis=-1)


def _pad_cols(w, width):
    return jnp.pad(w, [(0, 0)] * (w.ndim - 1) + [(0, width - w.shape[-1])])


def _prep_mla(w_in, q_norm, kv_norm, w_q_up, w_kv_up, w_out):
    w_kr = w_in[:, Q_LORA + KV_LORA:]
    w_in2 = jnp.concatenate([w_in[:, :Q_LORA + KV_LORA], _pad_cols(w_kr, LANES),
                             _pad_cols(_rot_half_cols(w_kr), LANES)], axis=1)
    wq = w_q_up.reshape(Q_LORA, MLA_HEADS, QK_NOPE + QK_ROPE)
    wr = wq[:, :, QK_NOPE:]
    hw = MLA_HEADS * LANES
    wkv = w_kv_up.reshape(KV_LORA, MLA_HEADS, QK_NOPE + V_HEAD)
    return {
        "w_in": w_in2.astype(BF16),
        "q_norm": q_norm.reshape(1, Q_LORA),
        "kv_norm": kv_norm.reshape(1, KV_LORA),
        "w_qn": wq[:, :, :QK_NOPE].reshape(Q_LORA, hw).astype(BF16),
        "w_qa": _pad_cols(wr, LANES).reshape(Q_LORA, hw).astype(BF16),
        "w_qb": _pad_cols(_rot_half_cols(wr), LANES).reshape(Q_LORA, hw).astype(BF16),
        "w_kv": w_kv_up.astype(BF16),
        "w_ukt": jnp.transpose(wkv[:, :, :QK_NOPE], (1, 2, 0)).astype(BF16),
        "w_uv": jnp.transpose(wkv[:, :, QK_NOPE:], (1, 0, 2)).astype(BF16),
        "w_out": w_out.astype(BF16),
    }


def _rope_tables(pos):
    inv = ROPE_THETA ** (-jnp.arange(0, QK_ROPE, 2, dtype=F32) / QK_ROPE)
    ang = pos.astype(F32)[:, None] * inv[None, :]
    reps = LANES // (QK_ROPE // 2)
    return jnp.tile(jnp.cos(ang), (1, reps)), jnp.tile(jnp.sin(ang), (1, reps))


def kernel(x_prompt, x_sample, cache_mla_ckv, cache_mla_krope, cache_diff_k, cache_diff_v, page_table, ffn_norm, w_ffn_up, w_ffn_down, mix_norm, mla_w_in, mla_q_norm, mla_kv_norm, mla_w_q_up, mla_w_kv_up, mla_w_out, diff_w_in, diff_lambda, diff_subln, diff_w_out, rel_bias, final_norm):
    xp = x_prompt.reshape(BATCH * SEQ, D_MODEL)
    xs = x_sample.reshape(DEC_BATCH, D_MODEL)
    w_up = w_ffn_up.astype(BF16)
    w_down = w_ffn_down.astype(BF16)
    gf = final_norm.reshape(1, D_MODEL)
    cos_p, sin_p = _rope_tables(jnp.arange(SEQ, dtype=jnp.int32))
    cos_s, sin_s = _rope_tables(jnp.full((1,), PAST_LEN, jnp.int32))
    n_phys = cache_mla_ckv.shape[1]

    def ffn(x, i, k, final=False, side=None):
        return _ffn(x, ffn_norm[i, k].reshape(1, D_MODEL), w_up, w_down, gf, i, k, final=final,
                    side=side)

    cache_krt = jnp.transpose(cache_mla_krope, (0, 1, 3, 2))
    cache_kt = jnp.transpose(cache_diff_k, (0, 1, 3, 4, 5, 2)).reshape(1, n_phys, DIFF_K, PAGE_SIZE)
    cache_v2 = cache_diff_v.reshape(1, n_phys, 2 * PAGE_SIZE, DIFF_VHEAD)
    half = DEC_BATCH // 2

    xs = ffn(xs, 0, 0)
    g0 = mix_norm[0].reshape(1, D_MODEL)
    wm = _prep_mla(mla_w_in[0], mla_q_norm[0], mla_kv_norm[0], mla_w_q_up[0], mla_w_kv_up[0],
                   mla_w_out[0])
    ckv_s, kr_s, qn_s, qr_s, _, _ = _mla_proj(xs, g0, wm, cos_s, sin_s)
    q_dec = jnp.transpose(_mla_qlat(qn_s, qr_s, wm["w_ukt"]), (1, 0, 2))

    def mla_side(row0):
        return _mla_side(page_table, row0, half, q_dec, ckv_s, kr_s, cache_mla_ckv, cache_krt)

    xp, o_lat_a = ffn(xp, 0, 0, side=mla_side(0))
    ckv_p, kr_p, qn_p, qr_p, kv_p, krb_p = _mla_proj(xp, g0, wm, cos_p, sin_p)
    o_p = _mla_attn(qn_p, qr_p, kv_p, krb_p)
    xp = _outproj(xp, o_p, wm["w_out"])
    xp, o_lat_b = ffn(xp, 0, 1, side=mla_side(half))

    o_lat = jnp.concatenate([o_lat_a, o_lat_b], axis=0)
    xs = _mla_finish(xs, jnp.transpose(o_lat, (1, 0, 2)), wm["w_uv"], wm["w_out"])
    xs = ffn(xs, 0, 1)

    xs = ffn(xs, 1, 0)
    lam_init = 0.8 - 0.6 * math.exp(-0.3 * 1)
    g1 = mix_norm[1].reshape(1, D_MODEL)
    w_din = diff_w_in[0].astype(BF16)
    w_dout = diff_w_out[0].astype(BF16)
    lam = diff_lambda[0]
    subln = diff_subln[0].reshape(1, DIFF_VHEAD)
    bias_tiles = _bias_tiles(rel_bias)
    dec_bias = _bias_decode(rel_bias)

    qkv_s = _diff_proj(xs, g1, w_din)
    q32 = qkv_s[:, :DIFF_Q].reshape(DEC_BATCH, DIFF_KV_HEADS, DIFF_GROUP, 2, DIFF_HEAD)
    q32 = jnp.transpose(q32, (0, 1, 3, 2, 4)).reshape(DEC_BATCH, 2 * DIFF_HEADS, DIFF_HEAD)
    k_s = qkv_s[:, DIFF_Q:DIFF_Q + DIFF_K]
    v_s = qkv_s[:, DIFF_Q + DIFF_K:]

    def diff_side(row0):
        return _diff_side(page_table, row0, half, q32, k_s, v_s, dec_bias, lam, subln, cache_kt,
                          cache_v2, lam_init)

    xp, a_s_a = ffn(xp, 1, 0, side=diff_side(0))
    qkv_p = _diff_proj(xp, g1, w_din)
    a_p = _diff_attn(qkv_p, bias_tiles, lam, subln, lam_init)
    xp = _outproj(xp, a_p, w_dout)
    xp, a_s_b = ffn(xp, 1, 1, final=True, side=diff_side(half))

    a_s = jnp.concatenate([a_s_a, a_s_b], axis=0)
    xs = _outproj(xs, a_s.reshape(DEC_BATCH, DIFF_HEADS * DIFF_VHEAD), w_dout)
    xs = ffn(xs, 1, 1, final=True)

    k_p = qkv_p[:, DIFF_Q:DIFF_Q + DIFF_K]
    v_p = qkv_p[:, DIFF_Q + DIFF_K:]
    return (
        xp.reshape(BATCH, SEQ, D_MODEL),
        xs.reshape(DEC_BATCH, 1, D_MODEL),
        ckv_p.reshape(1, BATCH, SEQ, KV_LORA),
        kr_p.reshape(1, BATCH, SEQ, QK_ROPE),
        k_p.reshape(1, BATCH, SEQ, DIFF_KV_HEADS, 2, DIFF_HEAD),
        v_p.reshape(1, BATCH, SEQ, DIFF_KV_HEADS, DIFF_VHEAD),
        ckv_s.reshape(1, DEC_BATCH, 1, KV_LORA),
        kr_s.reshape(1, DEC_BATCH, 1, QK_ROPE),
        k_s.reshape(1, DEC_BATCH, 1, DIFF_KV_HEADS, 2, DIFF_HEAD),
        v_s.reshape(1, DEC_BATCH, 1, DIFF_KV_HEADS, DIFF_VHEAD),
    )
```
